```python
import jax, jax.numpy as jnp
from jax import lax
import numpy as np

D_MODEL = 1024
BATCH = 2
SEQ = 8192
DEPTH = 2
DEC_BATCH = 32
DEC_SEQ = 4
PAST_LEN = 16384
PAGE_SIZE = 128

MIX_WIDTH = D_MODEL
GROUP_WIDTH = MIX_WIDTH // 4
HEAD_DIM = 64
GLA_HEADS = GROUP_WIDTH // HEAD_DIM
GLA_RANK = 16
GLA_TAU = 16.0
GLA_CHUNK = 64
FOX_HEADS = GROUP_WIDTH // HEAD_DIM
FOX_QBLOCK = 128
FORGET_BIAS = 3.0
CONV_CH = GROUP_WIDTH
CONV_WIDTH = 31
POOL_CH = GROUP_WIDTH
POOL_WINDOWS = (2, 4, 8, 16)
POOL_GROUPS = 4
POOL_GC = POOL_CH // POOL_GROUPS
POOL_MAX = 16
MOE_GROUPS = 4
EXPERTS_PER_GROUP = 4
N_EXPERTS = MOE_GROUPS * EXPERTS_PER_GROUP
TOP_K_INNER = 2
D_EXPERT = D_MODEL // 2
EPS = 1e-6

IN_SIZES = (GROUP_WIDTH, GROUP_WIDTH, GROUP_WIDTH, GROUP_WIDTH, GLA_RANK,
            GROUP_WIDTH, GROUP_WIDTH, GROUP_WIDTH, FOX_HEADS,
            CONV_CH, CONV_CH, POOL_CH)
IN_SPLITS = tuple(sum(IN_SIZES[:i + 1]) for i in range(len(IN_SIZES) - 1))
N_IN = sum(IN_SIZES)

kernel_name = 'hybrid_gla_fox_conv_pool_hmoe_step'

F32 = jnp.float32


def rms_norm(x, g):
    xf = x.astype(F32)
    y = xf * lax.rsqrt(jnp.mean(xf * xf, axis=-1, keepdims=True) + EPS)
    return (y * g.astype(F32)).astype(x.dtype)


def layer_norm(x, g, b):
    xf = x.astype(F32)
    mu = jnp.mean(xf, axis=-1, keepdims=True)
    var = jnp.mean(jnp.square(xf - mu), axis=-1, keepdims=True)
    y = (xf - mu) * lax.rsqrt(var + EPS) * g.astype(F32) + b.astype(F32)
    return y.astype(x.dtype)


def gla_chunk_step(S, inp):
    q, k, v, la = inp
    C = q.shape[2]
    b = jnp.cumsum(la, axis=2)
    inter = jnp.einsum('bhtd,bhde->bhte', q * jnp.exp(b), S)
    causal = jnp.tril(jnp.ones((C, C), dtype=bool))
    diff = b[:, :, :, None, :] - b[:, :, None, :, :]
    decay = jnp.exp(jnp.where(causal[:, :, None], diff, -jnp.inf))
    A = jnp.einsum('bhtd,bhsd,bhtsd->bhts', q, k, decay)
    o = inter + jnp.einsum('bhts,bhse->bhte', A, v)
    b_last = b[:, :, -1:, :]
    S_new = jnp.exp(b_last[:, :, 0, :])[..., None] * S + jnp.einsum('bhsd,bhse->bhde', k * jnp.exp(b_last - b), v)
    return S_new, o


def gla_run(S0, q, k, v, la):
    B, L, H, dk = q.shape
    C = GLA_CHUNK if L % GLA_CHUNK == 0 else L
    n = L // C

    def to_chunks(t):
        t = t.astype(F32).transpose(0, 2, 1, 3).reshape(B, H, n, C, t.shape[-1])
        return jnp.moveaxis(t, 2, 0)

    S, o = lax.scan(gla_chunk_step, S0.astype(F32),
                    (to_chunks(q * (dk ** -0.5)), to_chunks(k), to_chunks(v), to_chunks(la)))
    o = jnp.moveaxis(o, 0, 2).reshape(B, H, L, -1).transpose(0, 2, 1, 3)
    return o, S


def fox_attend(q, cq, qpos, k_all, v_all, c_allT):
    s = jnp.einsum('bqhd,bkhd->bhqk', q, k_all).astype(F32) * (HEAD_DIM ** -0.5)
    s = s + jnp.swapaxes(cq, 1, 2)[..., None] - c_allT[:, :, None, :]
    mask = jnp.arange(k_all.shape[1])[None, :] <= qpos[:, None]
    p = jax.nn.softmax(jnp.where(mask, s, -jnp.inf), axis=-1)
    return jnp.einsum('bhqk,bkhd->bqhd', p.astype(v_all.dtype), v_all)


def fox_run(q, k, v, log_f, past):
    if past is None:
        k_all, v_all, lf_all, P = k, v, log_f, 0
    else:
        kp, vp, lfp = past
        k_all = jnp.concatenate([kp, k], axis=1)
        v_all = jnp.concatenate([vp, v], axis=1)
        lf_all = jnp.concatenate([lfp.astype(F32), log_f], axis=1)
        P = kp.shape[1]
    c_all = jnp.cumsum(lf_all.astype(F32), axis=1)
    c_allT = jnp.swapaxes(c_all, 1, 2)
    cq = c_all[:, P:]
    B, L, H, dh = q.shape
    qpos = P + jnp.arange(L)
    if L % FOX_QBLOCK == 0 and L > FOX_QBLOCK:
        nb = L // FOX_QBLOCK
        qb = jnp.moveaxis(q.reshape(B, nb, FOX_QBLOCK, H, dh), 1, 0)
        cb = jnp.moveaxis(cq.reshape(B, nb, FOX_QBLOCK, H), 1, 0)
        pb = qpos.reshape(nb, FOX_QBLOCK)
        o = lax.map(lambda a: fox_attend(a[0], a[1], a[2], k_all, v_all, c_allT), (qb, cb, pb))
        o = jnp.moveaxis(o, 0, 1).reshape(B, L, H, dh)
    else:
        o = fox_attend(q, cq, qpos, k_all, v_all, c_allT)
    return o


def causal_dwconv(u_ext, w, b):
    y = lax.conv_general_dilated(u_ext, w[:, None, :], window_strides=(1,), padding='VALID',
                                 dimension_numbers=('NWC', 'WIO', 'NWC'),
                                 feature_group_count=u_ext.shape[-1])
    return y + b


def pool_mix(u, prefix, start, pool_w, pool_scale):
    B, L, _ = u.shape
    ext = jnp.concatenate([prefix.astype(u.dtype), u], axis=1)
    cs = jnp.cumsum(ext.astype(F32), axis=1)
    cs0 = jnp.concatenate([jnp.zeros((B, 1, POOL_CH), F32), cs], axis=1)
    pos = start + jnp.arange(L)
    means = []
    for g, w in enumerate(POOL_WINDOWS):
        sl = slice(g * POOL_GC, (g + 1) * POOL_GC)
        win_sum = cs0[:, POOL_MAX:POOL_MAX + L, sl] - cs0[:, POOL_MAX - w:POOL_MAX - w + L, sl]
        cnt = jnp.minimum(w, pos + 1).astype(F32)
        means.append(win_sum / cnt[None, :, None])
    d = (jnp.concatenate(means, axis=-1) - u.astype(F32)).reshape(B, L, POOL_GROUPS, POOL_GC)
    y = jnp.einsum('blgc,gcd->blgd', d, pool_w.astype(F32)).reshape(B, L, POOL_CH) * pool_scale.astype(F32)
    return y.astype(u.dtype), ext[:, -(POOL_MAX - 1):]


def hier_moe(h, wg, bg, we, be, w_gate, w_up, w_down):
    B, L, D = h.shape
    hf = h.reshape(B * L, D)
    g_logits = (hf @ wg + bg).astype(F32)
    _, g_idx = lax.top_k(g_logits, 1)
    p_group = jnp.take_along_axis(jax.nn.softmax(g_logits, axis=-1), g_idx, axis=-1)
    e_logits = (hf @ we + be).astype(F32).reshape(-1, MOE_GROUPS, EXPERTS_PER_GROUP)
    e_in = jnp.take_along_axis(e_logits, g_idx[:, :, None], axis=1)[:, 0]
    top_v, top_i = lax.top_k(e_in, TOP_K_INNER)
    gate = jax.nn.softmax(top_v, axis=-1) * p_group
    e_idx = g_idx * EXPERTS_PER_GROUP + top_i
    combine = jnp.sum(jax.nn.one_hot(e_idx, N_EXPERTS, dtype=F32) * gate[..., None], axis=1).astype(h.dtype)
    y = jnp.zeros_like(hf)
    for e in range(N_EXPERTS):
        he = jax.nn.silu(hf @ w_gate[e]) * (hf @ w_up[e])
        y = y + combine[:, e:e + 1] * (he @ w_down[e])
    return y.reshape(B, L, D)


def layer(x, p, gla_s0, conv_prefix, pool_prefix, fox_past, start):
    bsz, seq_len, _ = x.shape
    h = rms_norm(x, p['norm1_g'])
    z = h @ p['w_in']
    (g_q, g_k, g_v, g_g, g_a, f_q, f_k, f_v, f_f, c_a, c_b, p_u) = jnp.split(z, IN_SPLITS, axis=-1)

    def gheads(t):
        return t.reshape(bsz, seq_len, GLA_HEADS, HEAD_DIM)

    def fheads(t):
        return t.reshape(bsz, seq_len, FOX_HEADS, HEAD_DIM)

    log_alpha = jax.nn.log_sigmoid((g_a @ p['gla_w_a2'] + p['gla_b_a']).astype(F32)) / GLA_TAU
    o_gla, gla_state = gla_run(gla_s0, gheads(g_q), gheads(g_k), gheads(g_v), gheads(log_alpha))
    o_gla = rms_norm(o_gla.astype(x.dtype), p['gla_norm_g'].reshape(GLA_HEADS, HEAD_DIM)) * jax.nn.silu(gheads(g_g))
    o_gla = o_gla.reshape(bsz, seq_len, GROUP_WIDTH)

    fq = rms_norm(fheads(f_q), p['fox_q_norm_g'])
    fk = rms_norm(fheads(f_k), p['fox_k_norm_g'])
    fv = fheads(f_v)
    log_f = jax.nn.log_sigmoid((f_f + p['fox_b_f']).astype(F32))
    o_fox = fox_run(fq, fk, fv, log_f, fox_past).reshape(bsz, seq_len, GROUP_WIDTH)

    u = c_a * jax.nn.sigmoid(c_b)
    u_ext = jnp.concatenate([conv_prefix.astype(u.dtype), u], axis=1)
    o_conv = jax.nn.silu(layer_norm(causal_dwconv(u_ext, p['conv_w'], p['conv_b']), p['conv_ln_g'], p['conv_ln_b']))
    conv_state = u_ext[:, -(CONV_WIDTH - 1):]

    o_pool, pool_state = pool_mix(p_u, pool_prefix, start, p['pool_w'], p['pool_scale'])

    mix = jnp.concatenate([o_gla, o_fox, o_conv.astype(x.dtype), o_pool], axis=-1)
    x = x + mix @ p['w_out']
    x = x + hier_moe(rms_norm(x, p['norm2_g']), p['router_wg'], p['router_bg'], p['router_we'], p['router_be'],
                     p['exp_w_gate'], p['exp_w_up'], p['exp_w_down'])
    return x, (fk, fv, log_f.astype(x.dtype), gla_state.astype(x.dtype), conv_state, pool_state)


def setup_inputs(seed: int = 0) -> dict:
    key = jax.random.key(seed)
    ks = iter(jax.random.split(key, 48))

    def nrm(shape, scale):
        return jax.random.normal(next(ks), shape, F32) * scale

    n_pages = PAST_LEN // PAGE_SIZE
    n_used = DEC_BATCH * n_pages
    n_pool = n_used + max(1, n_used // 4)
    return {
        'x_prompt': nrm((BATCH, SEQ, D_MODEL), 1.0),
        'x_sample': nrm((DEC_BATCH, DEC_SEQ, D_MODEL), 1.0),
        'cache_k': nrm((DEPTH, n_pool, PAGE_SIZE, FOX_HEADS, HEAD_DIM), 1.0),
        'cache_v': nrm((DEPTH, n_pool, PAGE_SIZE, FOX_HEADS, HEAD_DIM), 1.0),
        'cache_logf': jax.nn.log_sigmoid(FORGET_BIAS + nrm((DEPTH, n_pool, PAGE_SIZE, FOX_HEADS), 1.0)),
        'page_table': jax.random.permutation(next(ks), n_pool)[:n_used].reshape(DEC_BATCH, n_pages).astype(jnp.int32),
        'state_gla': nrm((DEPTH, DEC_BATCH, GLA_HEADS, HEAD_DIM, HEAD_DIM), 0.5),
        'state_conv': nrm((DEPTH, DEC_BATCH, CONV_WIDTH - 1, CONV_CH), 0.5),
        'state_pool': nrm((DEPTH, DEC_BATCH, POOL_MAX - 1, POOL_CH), 1.0),
        'norm1_g': 1.0 + nrm((DEPTH, D_MODEL), 0.02),
        'w_in': nrm((DEPTH, D_MODEL, N_IN), D_MODEL ** -0.5),
        'gla_w_a2': nrm((DEPTH, GLA_RANK, GROUP_WIDTH), GLA_RANK ** -0.5),
        'gla_b_a': nrm((DEPTH, GROUP_WIDTH), 0.02),
        'gla_norm_g': 1.0 + nrm((DEPTH, GROUP_WIDTH), 0.02),
        'fox_b_f': FORGET_BIAS + nrm((DEPTH, FOX_HEADS), 0.1),
        'fox_q_norm_g': 1.0 + nrm((DEPTH, HEAD_DIM), 0.02),
        'fox_k_norm_g': 1.0 + nrm((DEPTH, HEAD_DIM), 0.02),
        'conv_w': nrm((DEPTH, CONV_WIDTH, CONV_CH), CONV_WIDTH ** -0.5),
        'conv_b': nrm((DEPTH, CONV_CH), 0.02),
        'conv_ln_g': 1.0 + nrm((DEPTH, CONV_CH), 0.02),
        'conv_ln_b': nrm((DEPTH, CONV_CH), 0.02),
        'pool_w': nrm((DEPTH, POOL_GROUPS, POOL_GC, POOL_GC), POOL_GC ** -0.5),
        'pool_scale': 1.0 + nrm((DEPTH, POOL_CH), 0.1),
        'w_out': nrm((DEPTH, MIX_WIDTH, D_MODEL), MIX_WIDTH ** -0.5),
        'norm2_g': 1.0 + nrm((DEPTH, D_MODEL), 0.02),
        'router_wg': nrm((DEPTH, D_MODEL, MOE_GROUPS), D_MODEL ** -0.5),
        'router_bg': nrm((DEPTH, MOE_GROUPS), 0.01),
        'router_we': nrm((DEPTH, D_MODEL, N_EXPERTS), D_MODEL ** -0.5),
        'router_be': nrm((DEPTH, N_EXPERTS), 0.01),
        'exp_w_gate': nrm((DEPTH, N_EXPERTS, D_MODEL, D_EXPERT), D_MODEL ** -0.5),
        'exp_w_up': nrm((DEPTH, N_EXPERTS, D_MODEL, D_EXPERT), D_MODEL ** -0.5),
        'exp_w_down': nrm((DEPTH, N_EXPERTS, D_EXPERT, D_MODEL), D_EXPERT ** -0.5),
    }


def reference(x_prompt, x_sample, cache_k, cache_v, cache_logf, page_table, state_gla, state_conv, state_pool,
              norm1_g, w_in, gla_w_a2, gla_b_a, gla_norm_g, fox_b_f, fox_q_norm_g, fox_k_norm_g,
              conv_w, conv_b, conv_ln_g, conv_ln_b, pool_w, pool_scale, w_out, norm2_g,
              router_wg, router_bg, router_we, router_be, exp_w_gate, exp_w_up, exp_w_down):
    dec_b, n_pages = page_table.shape
    past = n_pages * PAGE_SIZE
    bsz = x_prompt.shape[0]
    xp, xs = x_prompt, x_sample
    kp_l, vp_l, lfp_l, ks_l, vs_l, lfs_l = [], [], [], [], [], []
    glap_l, glas_l, convp_l, convs_l, poolp_l, pools_l = [], [], [], [], [], []
    for l in range(DEPTH):
        p = {
            'norm1_g': norm1_g[l], 'w_in': w_in[l], 'gla_w_a2': gla_w_a2[l], 'gla_b_a': gla_b_a[l],
            'gla_norm_g': gla_norm_g[l], 'fox_b_f': fox_b_f[l], 'fox_q_norm_g': fox_q_norm_g[l],
            'fox_k_norm_g': fox_k_norm_g[l], 'conv_w': conv_w[l], 'conv_b': conv_b[l],
            'conv_ln_g': conv_ln_g[l], 'conv_ln_b': conv_ln_b[l], 'pool_w': pool_w[l],
            'pool_scale': pool_scale[l], 'w_out': w_out[l], 'norm2_g': norm2_g[l],
            'router_wg': router_wg[l], 'router_bg': router_bg[l], 'router_we': router_we[l],
            'router_be': router_be[l], 'exp_w_gate': exp_w_gate[l], 'exp_w_up': exp_w_up[l],
            'exp_w_down': exp_w_down[l],
        }
        xp, sp = layer(xp, p,
                       jnp.zeros((bsz, GLA_HEADS, HEAD_DIM, HEAD_DIM), F32),
                       jnp.zeros((bsz, CONV_WIDTH - 1, CONV_CH), xp.dtype),
                       jnp.zeros((bsz, POOL_MAX - 1, POOL_CH), xp.dtype),
                       None, 0)
        fox_past = (cache_k[l][page_table].reshape(dec_b, past, FOX_HEADS, HEAD_DIM),
                    cache_v[l][page_table].reshape(dec_b, past, FOX_HEADS, HEAD_DIM),
                    cache_logf[l][page_table].reshape(dec_b, past, FOX_HEADS))
        xs, ss = layer(xs, p, state_gla[l], state_conv[l], state_pool[l], fox_past, past)
        kp_l.append(sp[0]); vp_l.append(sp[1]); lfp_l.append(sp[2])
        glap_l.append(sp[3]); convp_l.append(sp[4]); poolp_l.append(sp[5])
        ks_l.append(ss[0]); vs_l.append(ss[1]); lfs_l.append(ss[2])
        glas_l.append(ss[3]); convs_l.append(ss[4]); pools_l.append(ss[5])
    k_prompt = jnp.stack(kp_l); v_prompt = jnp.stack(vp_l); logf_prompt = jnp.stack(lfp_l)
    k_sample = jnp.stack(ks_l); v_sample = jnp.stack(vs_l); logf_sample = jnp.stack(lfs_l)
    gla_prompt = jnp.stack(glap_l); gla_sample = jnp.stack(glas_l)
    conv_prompt = jnp.stack(convp_l); conv_sample = jnp.stack(convs_l)
    pool_prompt = jnp.stack(poolp_l); pool_sample = jnp.stack(pools_l)
    return (xp, xs, k_prompt, v_prompt, logf_prompt, k_sample, v_sample, logf_sample,
            gla_prompt, gla_sample, conv_prompt, conv_sample, pool_prompt, pool_sample)
```

```python
import functools

import jax
import jax.numpy as jnp
import numpy as np
from jax import lax
from jax.experimental import pallas as pl
from jax.experimental.pallas import tpu as pltpu

F32 = jnp.float32
BF16 = jnp.bfloat16
HIGHEST = lax.Precision.HIGHEST

D_MODEL = 1024
GROUP_WIDTH = 256
HEAD_DIM = 64
N_HEADS = GROUP_WIDTH // HEAD_DIM
GLA_RANK = 16
GLA_TAU = 16.0
GLA_SUB = 16
PAGE_SIZE = 128
CONV_WIDTH = 31
CONV_HALO = 32
POOL_WINDOWS = (2, 4, 8, 16)
POOL_MAX = 16
MOE_GROUPS = 4
EXPERTS_PER_GROUP = 4
N_EXPERTS = MOE_GROUPS * EXPERTS_PER_GROUP
D_EXPERT = 512
EPS = 1e-6
LANES = 128
ROUTER_LANE0 = MOE_GROUPS
VMEM_LIMIT = 56 * 1024 * 1024

C_GLA, C_FQ, C_FK, C_FV, C_CA, C_CB, C_PU, C_SMALL, C_END = 0, 1024, 1280, 1536, 1792, 2048, 2304, 2560, 2688
SMALL_FF0 = 0
SMALL_GA0 = N_HEADS


def _cparams(sem):
    return pltpu.CompilerParams(dimension_semantics=sem, vmem_limit_bytes=VMEM_LIMIT)


def _log_sigmoid(x):
    return jnp.minimum(x, 0.0) - jnp.log1p(jnp.exp(-jnp.abs(x)))


_NN = (((1,), (0,)), ((), ()))
_NT = (((1,), (1,)), ((), ()))
_TN = (((0,), (0,)), ((), ()))


def _mm(a, b, precise, dims=_NN):
    if precise:
        return lax.dot_general(a.astype(F32), b.astype(F32), dims, precision=HIGHEST, preferred_element_type=F32)
    return lax.dot_general(a.astype(BF16), b.astype(BF16), dims, preferred_element_type=F32)


def _head_rms(x, bd_mean, g):
    ms = jnp.dot(x * x, bd_mean, precision=HIGHEST, preferred_element_type=F32)
    return x * lax.rsqrt(ms + EPS) * g


def _inproj_kernel(x_ref, g1_ref, w_ref, wa2_ref, ba_ref, bf_ref, gq_ref, gk_ref, bd_ref,
                   gla_ref, la_ref, fq_ref, fk_ref, fv_ref, lf_ref, u_ref, pu_ref, h_scr, *, precise):
    x = x_ref[...]
    ms = jnp.mean(x * x, axis=-1, keepdims=True)
    h_scr[...] = (x * lax.rsqrt(ms + EPS) * g1_ref[...]).astype(h_scr.dtype)

    def proj(c0, width):
        return _mm(h_scr[...], w_ref[:, c0:c0 + width], precise)

    for j in range(4):
        gla_ref[:, j * GROUP_WIDTH:(j + 1) * GROUP_WIDTH] = proj(C_GLA + j * GROUP_WIDTH, GROUP_WIDTH)
    small = proj(C_SMALL, LANES)
    la_pre = jnp.dot(small, wa2_ref[...], precision=HIGHEST, preferred_element_type=F32) + ba_ref[...]
    la_ref[...] = _log_sigmoid(la_pre) * (1.0 / GLA_TAU)
    lf_ref[...] = _log_sigmoid(small + bf_ref[...])
    bd = bd_ref[...]
    fq_ref[...] = _head_rms(proj(C_FQ, GROUP_WIDTH), bd, gq_ref[...])
    fk_ref[...] = _head_rms(proj(C_FK, GROUP_WIDTH), bd, gk_ref[...])
    fv_ref[...] = proj(C_FV, GROUP_WIDTH)
    u_ref[...] = proj(C_CA, GROUP_WIDTH) * jax.nn.sigmoid(proj(C_CB, GROUP_WIDTH))
    pu_ref[...] = proj(C_PU, GROUP_WIDTH)


def _inproj(x2d, lp, tm, precise):
    t = x2d.shape[0]
    row = lambda w: pl.BlockSpec((tm, w), lambda i: (i, 0))
    full = lambda a: pl.BlockSpec(a.shape, lambda i: (0,) * a.ndim)
    w_in = lp['w_in_f32'] if precise else lp['w_in']
    consts = (lp['g1'], w_in, lp['wa2'], lp['ba'], lp['bf'], lp['gq'], lp['gk'], lp['bd_mean'])
    widths = (D_MODEL, GROUP_WIDTH, GROUP_WIDTH, GROUP_WIDTH, GROUP_WIDTH, LANES, GROUP_WIDTH, GROUP_WIDTH)
    return pl.pallas_call(
        functools.partial(_inproj_kernel, precise=precise),
        grid=(t // tm,),
        in_specs=[row(D_MODEL)] + [full(a) for a in consts],
        out_specs=[row(w) for w in widths],
        out_shape=[jax.ShapeDtypeStruct((t, w), F32) for w in widths],
        scratch_shapes=[pltpu.VMEM((tm, D_MODEL), w_in.dtype)],
        compiler_params=_cparams(("arbitrary",)),
        name="inproj",
    )(x2d, *consts)


def _gla_kernel(qkvg_ref, la_ref, st0_ref, gn_ref, bd_ref, bdm_ref, tri_ref, o_ref, stf_ref,
                st_scr, b_scr, o_scr, *, ch, precise):
    c = pl.program_id(1)

    @pl.when(c == 0)
    def _():
        st_scr[...] = st0_ref[0]

    b_scr[...] = jnp.dot(tri_ref[...], la_ref[0], precision=HIGHEST, preferred_element_type=F32)
    rows = lax.broadcasted_iota(jnp.int32, (GLA_SUB, GROUP_WIDTH), 0)
    bd = bd_ref[...]
    mm_dtype = F32 if precise else BF16

    def step(i, carry):
        r0 = pl.multiple_of(i * GLA_SUB, GLA_SUB)
        q = qkvg_ref[0, pl.ds(r0, GLA_SUB), 0:GROUP_WIDTH] * (HEAD_DIM ** -0.5)
        k = qkvg_ref[0, pl.ds(r0, GLA_SUB), GROUP_WIDTH:2 * GROUP_WIDTH]
        v = qkvg_ref[0, pl.ds(r0, GLA_SUB), 2 * GROUP_WIDTH:3 * GROUP_WIDTH]
        b = b_scr[pl.ds(r0, GLA_SUB), :]
        b_last = b[GLA_SUB - 1:GLA_SUB, :]
        st = st_scr[...]
        o_inter = _mm(q * jnp.exp(b), st, precise, _NT)
        dst = _mm(v, k * jnp.exp(b_last - b), precise, _TN)
        st_scr[...] = st * jnp.exp(b_last) + dst * bd
        ws = []
        for s in range(GLA_SUB):
            decay = jnp.where(rows >= s, jnp.exp(b - b[s:s + 1, :]), 0.0)
            ws.append((decay * (q * k[s:s + 1, :])).astype(mm_dtype))
        a_b = _mm(jnp.concatenate(ws, axis=0), bd, precise)
        o_diag = a_b[0:GLA_SUB] * v[0:1, :]
        for s in range(1, GLA_SUB):
            o_diag = o_diag + a_b[s * GLA_SUB:(s + 1) * GLA_SUB] * v[s:s + 1, :]
        o_scr[pl.ds(r0, GLA_SUB), :] = o_inter + o_diag
        return carry

    lax.fori_loop(0, ch // GLA_SUB, step, 0)
    gate = qkvg_ref[0, :, 3 * GROUP_WIDTH:4 * GROUP_WIDTH]
    o_ref[0] = _head_rms(o_scr[...], bdm_ref[...], gn_ref[...]) * (gate * jax.nn.sigmoid(gate))

    @pl.when(c == pl.num_programs(1) - 1)
    def _():
        stf_ref[0] = st_scr[...]


def _gla(qkvg, la, st0, lp, ch, precise):
    b, l, _ = qkvg.shape
    tri = jnp.asarray(np.kron(np.eye(ch // GLA_SUB), np.tril(np.ones((GLA_SUB, GLA_SUB)))), F32)
    full = lambda a: pl.BlockSpec(a.shape, lambda i, j: (0,) * a.ndim)
    consts = (lp['gla_gn'], lp['bd_ones'], lp['bd_mean'], tri)
    return pl.pallas_call(
        functools.partial(_gla_kernel, ch=ch, precise=precise),
        grid=(b, l // ch),
        in_specs=[pl.BlockSpec((1, ch, 4 * GROUP_WIDTH), lambda i, j: (i, j, 0)),
                  pl.BlockSpec((1, ch, GROUP_WIDTH), lambda i, j: (i, j, 0)),
                  pl.BlockSpec((1, GROUP_WIDTH, GROUP_WIDTH), lambda i, j: (i, 0, 0))] + [full(a) for a in consts],
        out_specs=[pl.BlockSpec((1, ch, GROUP_WIDTH), lambda i, j: (i, j, 0)),
                   pl.BlockSpec((1, GROUP_WIDTH, GROUP_WIDTH), lambda i, j: (i, 0, 0))],
        out_shape=[jax.ShapeDtypeStruct((b, l, GROUP_WIDTH), F32),
                   jax.ShapeDtypeStruct((b, GROUP_WIDTH, GROUP_WIDTH), F32)],
        scratch_shapes=[pltpu.VMEM((GROUP_WIDTH, GROUP_WIDTH), F32),
                        pltpu.VMEM((ch, GROUP_WIDTH), F32),
                        pltpu.VMEM((ch, GROUP_WIDTH), F32)],
        compiler_params=_cparams(("arbitrary", "arbitrary")),
        name="gla",
    )(qkvg, la, st0, *consts)


def _state_to_bd(s):
    b = s.shape[0]
    st = jnp.swapaxes(s.astype(F32), 2, 3)
    out = jnp.zeros((b, N_HEADS, HEAD_DIM, N_HEADS, HEAD_DIM), F32)
    for h in range(N_HEADS):
        out = out.at[:, h, :, h, :].set(st[:, h])
    return out.reshape(b, GROUP_WIDTH, GROUP_WIDTH)


def _bd_to_state(st):
    b = st.shape[0]
    s5 = st.reshape(b, N_HEADS, HEAD_DIM, N_HEADS, HEAD_DIM)
    diag = jnp.stack([s5[:, h, :, h, :] for h in range(N_HEADS)], axis=1)
    return jnp.swapaxes(diag, 2, 3)


def _cumsum_kernel(lf_ref, tri_ref, c_ref, ct_ref, carry_scr):
    @pl.when(pl.program_id(1) == 0)
    def _():
        carry_scr[...] = jnp.zeros_like(carry_scr)

    c = jnp.dot(tri_ref[...], lf_ref[0], precision=HIGHEST, preferred_element_type=F32) + carry_scr[...]
    carry_scr[...] = c[c.shape[0] - 1:, :]
    c_ref[0] = c
    ct_ref[0] = c.T[0:8, :]


def _fox_cumsum(lf, tm):
    b, l, _ = lf.shape
    tri = jnp.asarray(np.tril(np.ones((tm, tm))), F32)
    return pl.pallas_call(
        _cumsum_kernel,
        grid=(b, l // tm),
        in_specs=[pl.BlockSpec((1, tm, LANES), lambda i, j: (i, j, 0)),
                  pl.BlockSpec((tm, tm), lambda i, j: (0, 0))],
        out_specs=[pl.BlockSpec((1, tm, LANES), lambda i, j: (i, j, 0)),
                   pl.BlockSpec((1, 8, tm), lambda i, j: (i, 0, j))],
        out_shape=[jax.ShapeDtypeStruct((b, l, LANES), F32), jax.ShapeDtypeStruct((b, 8, l), F32)],
        scratch_shapes=[pltpu.VMEM((1, LANES), F32)],
        compiler_params=_cparams(("arbitrary", "arbitrary")),
        name="fox_cumsum",
    )(lf, tri)


def _fox_prompt_kernel(q_ref, k_ref, v_ref, c_ref, ct_ref, o_ref, m_scr, l_scr, acc_scr, *, tq):
    qi = pl.program_id(1)
    ki = pl.program_id(2)

    @pl.when(ki == 0)
    def _():
        m_scr[...] = jnp.full_like(m_scr, -jnp.inf)
        l_scr[...] = jnp.zeros_like(l_scr)
        acc_scr[...] = jnp.zeros_like(acc_scr)

    def block(masked):
        for h in range(N_HEADS):
            hs = slice(h * HEAD_DIM, (h + 1) * HEAD_DIM)
            q = (q_ref[0, :, hs] * (HEAD_DIM ** -0.5)).astype(BF16)
            s = lax.dot_general(q, k_ref[0, :, hs].astype(BF16), (((1,), (1,)), ((), ())),
                                preferred_element_type=F32)
            s = s + c_ref[0, :, h:h + 1] - ct_ref[0, h:h + 1, :]
            if masked:
                row = lax.broadcasted_iota(jnp.int32, (tq, tq), 0)
                col = lax.broadcasted_iota(jnp.int32, (tq, tq), 1)
                s = jnp.where(row >= col, s, -jnp.inf)
            m_prev = m_scr[h]
            m_new = jnp.maximum(m_prev, jnp.max(s, axis=-1, keepdims=True))
            alpha = jnp.exp(m_prev - m_new)
            p = jnp.exp(s - m_new)
            l_scr[h] = alpha * l_scr[h] + jnp.sum(p, axis=-1, keepdims=True)
            acc_scr[h] = alpha * acc_scr[h] + jnp.dot(p.astype(BF16), v_ref[0, :, hs].astype(BF16),
                                                      preferred_element_type=F32)
            m_scr[h] = m_new

    @pl.when(ki < qi)
    def _():
        block(False)

    @pl.when(ki == qi)
    def _():
        block(True)
        for h in range(N_HEADS):
            o_ref[0, :, h * HEAD_DIM:(h + 1) * HEAD_DIM] = acc_scr[h] / l_scr[h]


def _fox_prompt(fq, fk, fv, c, ct, tq):
    b, l, _ = fq.shape
    n = l // tq
    qmap = lambda i, j, k: (i, j, 0)
    kmap = lambda i, j, k: (i, jnp.minimum(k, j), 0)
    return pl.pallas_call(
        functools.partial(_fox_prompt_kernel, tq=tq),
        grid=(b, n, n),
        in_specs=[pl.BlockSpec((1, tq, GROUP_WIDTH), qmap),
                  pl.BlockSpec((1, tq, GROUP_WIDTH), kmap),
                  pl.BlockSpec((1, tq, GROUP_WIDTH), kmap),
                  pl.BlockSpec((1, tq, LANES), qmap),
                  pl.BlockSpec((1, 8, tq), lambda i, j, k: (i, 0, jnp.minimum(k, j)))],
        out_specs=pl.BlockSpec((1, tq, GROUP_WIDTH), qmap),
        out_shape=jax.ShapeDtypeStruct((b, l, GROUP_WIDTH), F32),
        scratch_shapes=[pltpu.VMEM((N_HEADS, tq, 1), F32), pltpu.VMEM((N_HEADS, tq, 1), F32),
                        pltpu.VMEM((N_HEADS, tq, HEAD_DIM), F32)],
        compiler_params=_cparams(("arbitrary", "arbitrary", "arbitrary")),
        name="fox_prompt",
    )(fq, fk, fv, c, ct)


PAGES_PER_CHUNK = 8


def _fox_paged_kernel(pt_ref, q_ref, kn_ref, vn_ref, lfn_ref, mc_ref, mtot_ref, trix_ref, hmask_ref,
                      ck_hbm, cv_hbm, clf_hbm, o_ref,
                      lf_scr, c_scr, kbuf, vbuf, kn_scr, vn_scr, m_scr, l_scr, acc_scr, sem_lf, sem_kv,
                      *, n_pages, page0):
    bi = pl.program_id(0)
    n_chunks = n_pages // PAGES_PER_CHUNK
    n_new = q_ref.shape[1]
    n_rows = N_HEADS * n_new

    def lf_copy(j):
        pg = pt_ref[bi, j] + page0
        return pltpu.make_async_copy(clf_hbm.at[pl.ds(pg, 1), :], lf_scr.at[pl.ds(j, 1), :], sem_lf.at[0])

    def kv_copies(chunk, slot, p):
        pg = pt_ref[bi, chunk * PAGES_PER_CHUNK + p] + page0
        return (pltpu.make_async_copy(ck_hbm.at[pg], kbuf.at[slot, p], sem_kv.at[0, slot]),
                pltpu.make_async_copy(cv_hbm.at[pg], vbuf.at[slot, p], sem_kv.at[1, slot]))

    def start_chunk(chunk, slot):
        for p in range(PAGES_PER_CHUNK):
            for cp in kv_copies(chunk, slot, p):
                cp.start()

    def wait_chunk(chunk, slot):
        for p in range(PAGES_PER_CHUNK):
            for cp in kv_copies(chunk, slot, p):
                cp.wait()

    def lf_start(j, carry):
        lf_copy(j).start()
        return carry

    def lf_wait(j, carry):
        lf_copy(j).wait()
        return carry

    lax.fori_loop(0, n_pages, lf_start, 0)
    start_chunk(0, 0)

    lf_scr[n_pages:, :] = jnp.zeros((lf_scr.shape[0] - n_pages, lf_scr.shape[1]), F32)
    lf_scr[n_pages:n_pages + 1, :] = lfn_ref[0]
    kn_scr[...] = jnp.zeros_like(kn_scr)
    vn_scr[...] = jnp.zeros_like(vn_scr)
    kn_scr[0:n_new, :] = kn_ref[0]
    vn_scr[0:n_new, :] = vn_ref[0]
    m_scr[...] = jnp.full_like(m_scr, -jnp.inf)
    l_scr[...] = jnp.zeros_like(l_scr)
    acc_scr[...] = jnp.zeros_like(acc_scr)

    lax.fori_loop(0, n_pages, lf_wait, 0)
    lf = lf_scr[...]
    c_local = jnp.dot(lf, mc_ref[...], precision=HIGHEST, preferred_element_type=F32)
    tot = jnp.dot(lf, mtot_ref[...], precision=HIGHEST, preferred_element_type=F32)
    c_scr[...] = c_local + jnp.dot(trix_ref[...], tot, precision=HIGHEST, preferred_element_type=F32)

    hmask = hmask_ref[...]
    q4 = q_ref[0] * (HEAD_DIM ** -0.5)
    qbd = (jnp.concatenate([q4] * N_HEADS, axis=0) * hmask).astype(BF16)
    lane = lax.broadcasted_iota(jnp.int32, (n_rows, PAGE_SIZE), 1)
    tok = lax.broadcasted_iota(jnp.int32, (n_rows, PAGE_SIZE), 0) % n_new

    def c_rows(j):
        crow = c_scr[pl.ds(j, 1), :]
        return jnp.concatenate(
            [jnp.broadcast_to(crow[:, h * PAGE_SIZE:(h + 1) * PAGE_SIZE], (n_new, PAGE_SIZE))
             for h in range(N_HEADS)], axis=0)

    cq = jnp.sum(jnp.where(lane == tok, c_rows(n_pages), 0.0), axis=-1, keepdims=True)

    def attend(k_page, v_page, j, masked):
        s = lax.dot_general(qbd, k_page.astype(BF16), (((1,), (1,)), ((), ())), preferred_element_type=F32)
        s = s + cq - c_rows(j)
        if masked:
            s = jnp.where(lane <= tok, s, -jnp.inf)
        m_prev = m_scr[...]
        m_new = jnp.maximum(m_prev, jnp.max(s, axis=-1, keepdims=True))
        alpha = jnp.exp(m_prev - m_new)
        p = jnp.exp(s - m_new)
        l_scr[...] = alpha * l_scr[...] + jnp.sum(p, axis=-1, keepdims=True)
        acc_scr[...] = alpha * acc_scr[...] + jnp.dot(p.astype(BF16), v_page.astype(BF16),
                                                      preferred_element_type=F32)
        m_scr[...] = m_new

    def chunk_body(chunk, carry):
        slot = chunk % 2
        wait_chunk(chunk, slot)

        @pl.when(chunk + 1 < n_chunks)
        def _():
            start_chunk(chunk + 1, 1 - slot)

        for p in range(PAGES_PER_CHUNK):
            attend(kbuf[slot, p], vbuf[slot, p], chunk * PAGES_PER_CHUNK + p, False)
        return carry

    lax.fori_loop(0, n_chunks, chunk_body, 0)
    attend(kn_scr[...], vn_scr[...], n_pages, True)

    o_full = acc_scr[...] / l_scr[...] * hmask
    o = o_full[0:n_new]
    for h in range(1, N_HEADS):
        o = o + o_full[h * n_new:(h + 1) * n_new]
    o_ref[0] = o


def _fox_paged(page_table, fq, fk, fv, lf_new_row, cache_k, cache_v, cache_lf, layer):
    s, n_new, _ = fq.shape
    n_pages = page_table.shape[1]
    n_pool = cache_k.shape[0] // 2
    n_rows = N_HEADS * n_new
    c_rows = ((n_pages + 1 + 7) // 8) * 8
    lanes = PAGE_SIZE * N_HEADS
    src = np.arange(lanes)
    dst = np.arange(lanes)
    same_head = (src[:, None] % N_HEADS) == (dst[None, :] // PAGE_SIZE)
    mc = jnp.asarray(same_head & ((src[:, None] // N_HEADS) <= (dst[None, :] % PAGE_SIZE)), F32)
    mtot = jnp.asarray(same_head, F32)
    trix = jnp.asarray(np.tril(np.ones((c_rows, c_rows)), -1), F32)
    hmask = jnp.asarray((np.arange(n_rows)[:, None] // n_new) == (np.arange(GROUP_WIDTH)[None, :] // HEAD_DIM), F32)
    tokspec = lambda w: pl.BlockSpec((1, n_new, w), lambda i, pt: (i, 0, 0))
    full = lambda a: pl.BlockSpec(a.shape, lambda i, pt: (0,) * a.ndim)
    anyspec = pl.BlockSpec(memory_space=pl.ANY)
    grid_spec = pltpu.PrefetchScalarGridSpec(
        num_scalar_prefetch=1,
        grid=(s,),
        in_specs=[tokspec(GROUP_WIDTH), tokspec(GROUP_WIDTH), tokspec(GROUP_WIDTH),
                  pl.BlockSpec((1, 1, lanes), lambda i, pt: (i, 0, 0)),
                  full(mc), full(mtot), full(trix), full(hmask), anyspec, anyspec, anyspec],
        out_specs=tokspec(GROUP_WIDTH),
        scratch_shapes=[pltpu.VMEM((c_rows, lanes), F32), pltpu.VMEM((c_rows, lanes), F32),
                        pltpu.VMEM((2, PAGES_PER_CHUNK, PAGE_SIZE, GROUP_WIDTH), F32),
                        pltpu.VMEM((2, PAGES_PER_CHUNK, PAGE_SIZE, GROUP_WIDTH), F32),
                        pltpu.VMEM((PAGE_SIZE, GROUP_WIDTH), F32), pltpu.VMEM((PAGE_SIZE, GROUP_WIDTH), F32),
                        pltpu.VMEM((n_rows, 1), F32), pltpu.VMEM((n_rows, 1), F32),
                        pltpu.VMEM((n_rows, GROUP_WIDTH), F32),
                        pltpu.SemaphoreType.DMA((1,)), pltpu.SemaphoreType.DMA((2, 2))],
    )
    return pl.pallas_call(
        functools.partial(_fox_paged_kernel, n_pages=n_pages, page0=layer * n_pool),
        grid_spec=grid_spec,
        out_shape=jax.ShapeDtypeStruct((s, n_new, GROUP_WIDTH), F32),
        compiler_params=_cparams(("arbitrary",)),
        name="fox_paged",
    )(page_table, fq, fk, fv, lf_new_row, mc, mtot, trix, hmask, cache_k, cache_v, cache_lf)


def _convpool_kernel(u_ref, pu_ref, cpre_ref, ppre_ref, cw_ref, cb_ref, lng_ref, lnb_ref, pw_ref, ps_ref,
                     oc_ref, op_ref, cext, pext, *, tm, pos0, precise):
    i = pl.program_id(1)

    @pl.when(i == 0)
    def _():
        cext[0:CONV_HALO, :] = cpre_ref[0]
        pext[0:POOL_MAX, :] = ppre_ref[0]

    @pl.when(i > 0)
    def _():
        cext[0:CONV_HALO, :] = cext[tm:tm + CONV_HALO, :]
        pext[0:POOL_MAX, :] = pext[tm:tm + POOL_MAX, :]

    cext[CONV_HALO:CONV_HALO + tm, :] = u_ref[0]
    pext[POOL_MAX:POOL_MAX + tm, :] = pu_ref[0]

    off = CONV_HALO - (CONV_WIDTH - 1)
    y = cext[off:off + tm, :] * cw_ref[0:1, :]
    for j in range(1, CONV_WIDTH):
        y = y + cext[off + j:off + j + tm, :] * cw_ref[j:j + 1, :]
    y = y + cb_ref[...]
    mu = jnp.mean(y, axis=-1, keepdims=True)
    yc = y - mu
    var = jnp.mean(yc * yc, axis=-1, keepdims=True)
    yn = yc * lax.rsqrt(var + EPS) * lng_ref[...] + lnb_ref[...]
    oc_ref[0] = yn * jax.nn.sigmoid(yn)

    lane_group = lax.broadcasted_iota(jnp.int32, (tm, GROUP_WIDTH), 1) // (GROUP_WIDTH // len(POOL_WINDOWS))
    pos = pos0 + i * tm + lax.broadcasted_iota(jnp.int32, (tm, 1), 0)
    x0 = pext[POOL_MAX:POOL_MAX + tm, :]
    run = x0
    mean = jnp.zeros((tm, GROUP_WIDTH), F32)
    for back in range(1, POOL_MAX):
        run = run + pext[POOL_MAX - back:POOL_MAX - back + tm, :]
        if back + 1 in POOL_WINDOWS:
            g = POOL_WINDOWS.index(back + 1)
            cnt = jnp.minimum(back + 1, pos + 1).astype(F32)
            mean = jnp.where(lane_group == g, run / cnt, mean)
    d = mean - x0
    op_ref[0] = _mm(d, pw_ref[...], precise) * ps_ref[...]


def _convpool(u, pu, conv_prefix, pool_prefix, lp, tm, pos0, precise):
    b, l, _ = u.shape
    seq = pl.BlockSpec((1, tm, GROUP_WIDTH), lambda i, j: (i, j, 0))
    full = lambda a: pl.BlockSpec(a.shape, lambda i, j: (0,) * a.ndim)
    consts = (lp['conv_w'], lp['conv_b'], lp['conv_ln_g'], lp['conv_ln_b'],
              lp['pool_w_f32'] if precise else lp['pool_w'], lp['pool_scale'])
    return pl.pallas_call(
        functools.partial(_convpool_kernel, tm=tm, pos0=pos0, precise=precise),
        grid=(b, l // tm),
        in_specs=[seq, seq,
                  pl.BlockSpec((1, CONV_HALO, GROUP_WIDTH), lambda i, j: (i, 0, 0)),
                  pl.BlockSpec((1, POOL_MAX, GROUP_WIDTH), lambda i, j: (i, 0, 0))] + [full(a) for a in consts],
        out_specs=[seq, seq],
        out_shape=[jax.ShapeDtypeStruct((b, l, GROUP_WIDTH), F32)] * 2,
        scratch_shapes=[pltpu.VMEM((CONV_HALO + tm, GROUP_WIDTH), F32), pltpu.VMEM((POOL_MAX + tm, GROUP_WIDTH), F32)],
        compiler_params=_cparams(("arbitrary", "arbitrary")),
        name="convpool",
    )(u, pu, conv_prefix, pool_prefix, *consts)


def _router_logits(h2, wr):
    return jnp.dot(h2, wr, precision=HIGHEST, preferred_element_type=F32)


def _outproj_kernel(x_ref, og_ref, of_ref, oc_ref, op_ref, w_ref, g2_ref, wr_ref, br_ref,
                    x1_ref, h2_ref, lg_ref, *, precise):
    acc = x_ref[...]
    for j, m_ref in enumerate((og_ref, of_ref, oc_ref, op_ref)):
        acc = acc + _mm(m_ref[...], w_ref[j * GROUP_WIDTH:(j + 1) * GROUP_WIDTH, :], precise)
    x1_ref[...] = acc
    ms = jnp.mean(acc * acc, axis=-1, keepdims=True)
    h2 = acc * lax.rsqrt(ms + EPS) * g2_ref[...]
    h2_ref[...] = h2.astype(h2_ref.dtype)
    lg_ref[...] = _router_logits(h2, wr_ref[...]) + br_ref[...]


def _outproj(x2d, og, of, oc, op, lp, tm, precise):
    t = x2d.shape[0]
    row = lambda w: pl.BlockSpec((tm, w), lambda i: (i, 0))
    full = lambda a: pl.BlockSpec(a.shape, lambda i: (0,) * a.ndim)
    consts = (lp['w_out_f32'] if precise else lp['w_out'], lp['g2'], lp['w_router'], lp['b_router'])
    return pl.pallas_call(
        functools.partial(_outproj_kernel, precise=precise),
        grid=(t // tm,),
        in_specs=[row(D_MODEL)] + [row(GROUP_WIDTH)] * 4 + [full(a) for a in consts],
        out_specs=[row(D_MODEL), row(D_MODEL), row(LANES)],
        out_shape=[jax.ShapeDtypeStruct((t, D_MODEL), F32),
                   jax.ShapeDtypeStruct((t, D_MODEL), F32 if precise else BF16),
                   jax.ShapeDtypeStruct((t, LANES), F32)],
        compiler_params=_cparams(("arbitrary",)),
        name="outproj",
    )(x2d, og, of, oc, op, *consts)


def _route(lg):
    lane = lax.broadcasted_iota(jnp.int32, lg.shape, 1)
    big = jnp.int32(LANES)
    neg = -jnp.inf
    is_g = lane < MOE_GROUPS
    gl = jnp.where(is_g, lg, neg)
    gmax = jnp.max(gl, axis=-1, keepdims=True)
    gidx = jnp.min(jnp.where(gl == gmax, lane, big), axis=-1, keepdims=True)
    p_group = 1.0 / jnp.sum(jnp.where(is_g, jnp.exp(lg - gmax), 0.0), axis=-1, keepdims=True)
    lo = ROUTER_LANE0 + gidx * EXPERTS_PER_GROUP
    el = jnp.where((lane >= lo) & (lane < lo + EXPERTS_PER_GROUP), lg, neg)
    v1 = jnp.max(el, axis=-1, keepdims=True)
    i1 = jnp.min(jnp.where(el == v1, lane, big), axis=-1, keepdims=True)
    el2 = jnp.where(lane == i1, neg, el)
    v2 = jnp.max(el2, axis=-1, keepdims=True)
    i2 = jnp.min(jnp.where(el2 == v2, lane, big), axis=-1, keepdims=True)
    e21 = jnp.exp(v2 - v1)
    g1 = 1.0 / (1.0 + e21)
    g2 = e21 * g1
    return jnp.where(lane == i1, g1 * p_group, 0.0) + jnp.where(lane == i2, g2 * p_group, 0.0)


def _moe_dense_kernel(h_ref, lg_ref, x1_ref, wg_ref, wu_ref, wd_ref, o_ref, comb_scr, acc_scr, *, precise):
    e = pl.program_id(1)

    @pl.when(e == 0)
    def _():
        comb_scr[...] = _route(lg_ref[...])
        acc_scr[...] = x1_ref[...]

    h = h_ref[...]
    a = _mm(h, wg_ref[0], precise)
    b = _mm(h, wu_ref[0], precise)
    y = _mm(a * jax.nn.sigmoid(a) * b, wd_ref[0], precise)
    lane = lax.broadcasted_iota(jnp.int32, comb_scr.shape, 1)
    cw = jnp.sum(jnp.where(lane == e + ROUTER_LANE0, comb_scr[...], 0.0), axis=-1, keepdims=True)
    acc_scr[...] = acc_scr[...] + cw * y

    @pl.when(e == N_EXPERTS - 1)
    def _():
        o_ref[...] = acc_scr[...]


def _moe_dense(h2, lg, x1, lp, tm, precise):
    t = h2.shape[0]
    row = lambda w: pl.BlockSpec((tm, w), lambda i, e: (i, 0))
    sfx = '_f32' if precise else ''
    return pl.pallas_call(
        functools.partial(_moe_dense_kernel, precise=precise),
        grid=(t // tm, N_EXPERTS),
        in_specs=[row(D_MODEL), row(LANES), row(D_MODEL),
                  pl.BlockSpec((1, D_MODEL, D_EXPERT), lambda i, e: (e, 0, 0)),
                  pl.BlockSpec((1, D_MODEL, D_EXPERT), lambda i, e: (e, 0, 0)),
                  pl.BlockSpec((1, D_EXPERT, D_MODEL), lambda i, e: (e, 0, 0))],
        out_specs=row(D_MODEL),
        out_shape=jax.ShapeDtypeStruct((t, D_MODEL), F32),
        scratch_shapes=[pltpu.VMEM((tm, LANES), F32), pltpu.VMEM((tm, D_MODEL), F32)],
        compiler_params=_cparams(("arbitrary", "arbitrary")),
        name="moe_dense",
    )(h2, lg, x1, lp['exp_w_gate' + sfx], lp['exp_w_up' + sfx], lp['exp_w_down' + sfx])


def _block_diag(blocks):
    n, r, c = blocks.shape
    out = jnp.zeros((n, r, n, c), blocks.dtype)
    for g in range(n):
        out = out.at[g, :, g, :].set(blocks[g])
    return out.reshape(n * r, n * c)


def _prep_layer(l, p):
    w = p['w_in'][l]
    o = np.cumsum((0, 256, 256, 256, 256, GLA_RANK, 256, 256, 256, N_HEADS, 256, 256, 256))
    small = jnp.concatenate([w[:, o[8]:o[9]], w[:, o[4]:o[5]],
                             jnp.zeros((D_MODEL, LANES - N_HEADS - GLA_RANK), F32)], axis=1)
    w_packed = jnp.concatenate([w[:, o[0]:o[4]], w[:, o[5]:o[8]], w[:, o[9]:o[12]], small], axis=1)
    wa2 = jnp.zeros((LANES, GROUP_WIDTH), F32).at[SMALL_GA0:SMALL_GA0 + GLA_RANK].set(p['gla_w_a2'][l])
    bf = jnp.zeros((1, LANES), F32).at[0, SMALL_FF0:SMALL_FF0 + N_HEADS].set(p['fox_b_f'][l])
    ones_bd = _block_diag(jnp.ones((N_HEADS, HEAD_DIM, HEAD_DIM), F32))
    w_router = jnp.concatenate([p['router_wg'][l], p['router_we'][l],
                                jnp.zeros((D_MODEL, LANES - MOE_GROUPS - N_EXPERTS), F32)], axis=1)
    b_router = jnp.concatenate([p['router_bg'][l], p['router_be'][l],
                                jnp.zeros((LANES - MOE_GROUPS - N_EXPERTS,), F32)])[None, :]
    conv_w = jnp.concatenate([p['conv_w'][l], jnp.zeros((CONV_HALO - CONV_WIDTH, GROUP_WIDTH), F32)], axis=0)
    row = lambda a: a.reshape(1, -1).astype(F32)
    return {
        'g1': row(p['norm1_g'][l]), 'w_in': w_packed.astype(BF16), 'wa2': wa2, 'ba': row(p['gla_b_a'][l]), 'bf': bf,
        'gq': row(jnp.tile(p['fox_q_norm_g'][l], N_HEADS)), 'gk': row(jnp.tile(p['fox_k_norm_g'][l], N_HEADS)),
        'bd_mean': ones_bd * (1.0 / HEAD_DIM), 'bd_ones': ones_bd, 'gla_gn': row(p['gla_norm_g'][l]),
        'conv_w': conv_w, 'conv_b': row(p['conv_b'][l]), 'conv_ln_g': row(p['conv_ln_g'][l]),
        'conv_ln_b': row(p['conv_ln_b'][l]), 'pool_w': _block_diag(p['pool_w'][l]).astype(BF16),
        'pool_scale': row(p['pool_scale'][l]), 'w_out': p['w_out'][l].astype(BF16), 'g2': row(p['norm2_g'][l]),
        'w_router': w_router, 'b_router': b_router,
        'exp_w_gate': p['exp_w_gate'][l].astype(BF16), 'exp_w_up': p['exp_w_up'][l].astype(BF16),
        'exp_w_down': p['exp_w_down'][l].astype(BF16),
        'w_in_f32': w_packed, 'pool_w_f32': _block_diag(p['pool_w'][l].astype(F32)), 'w_out_f32': p['w_out'][l],
        'exp_w_gate_f32': p['exp_w_gate'][l], 'exp_w_up_f32': p['exp_w_up'][l], 'exp_w_down_f32': p['exp_w_down'][l],
    }


def _pad_rows_front(a, rows):
    return jnp.pad(a.astype(F32), ((0, 0), (rows - a.shape[1], 0), (0, 0)))


def _flat(a):
    return a.reshape(-1, a.shape[-1])


def _layer_prompt(lp, xp, bsz, seq):
    tp = 512
    gla_p, la_p, fq_p, fk_p, fv_p, lf_p, u_p, pu_p = _inproj(xp, lp, tp, False)
    seq3 = lambda a: a.reshape(bsz, seq, a.shape[-1])
    og_p, st_p = _gla(seq3(gla_p), seq3(la_p), jnp.zeros((bsz, GROUP_WIDTH, GROUP_WIDTH), F32), lp, 128, False)
    c_p, ct_p = _fox_cumsum(seq3(lf_p), 512)
    of_p = _fox_prompt(seq3(fq_p), seq3(fk_p), seq3(fv_p), c_p, ct_p, 512)
    oc_p, op_p = _convpool(seq3(u_p), seq3(pu_p), jnp.zeros((bsz, CONV_HALO, GROUP_WIDTH), F32),
                           jnp.zeros((bsz, POOL_MAX, GROUP_WIDTH), F32), lp, 512, 0, False)
    x1_p, h2_p, lg_p = _outproj(xp, _flat(og_p), _flat(of_p), _flat(oc_p), _flat(op_p), lp, tp, False)
    xp_new = _moe_dense(h2_p, lg_p, x1_p, lp, 1024, False)
    heads = lambda a: a.reshape(bsz, seq, N_HEADS, HEAD_DIM)
    outs = (heads(fk_p), heads(fv_p), seq3(lf_p)[:, :, :N_HEADS], _bd_to_state(st_p),
            seq3(u_p)[:, -(CONV_WIDTH - 1):], seq3(pu_p)[:, -(POOL_MAX - 1):])
    return xp_new, outs


def _layer_sample(l, lp, xs, dec_b, dec_seq, page_table, caches, state_gla, state_conv, state_pool):
    cache_k, cache_v, cache_lf = caches
    past = page_table.shape[1] * PAGE_SIZE
    flat = _flat
    ts = xs.shape[0]
    gla_s, la_s, fq_s, fk_s, fv_s, lf_s, u_s, pu_s = _inproj(xs, lp, ts, True)
    dec3 = lambda a: a.reshape(dec_b, dec_seq, a.shape[-1])
    pad_t = lambda a: jnp.pad(dec3(a), ((0, 0), (0, GLA_SUB - dec_seq), (0, 0)))
    og_s, st_s = _gla(pad_t(gla_s), pad_t(la_s), _state_to_bd(state_gla[l]), lp, GLA_SUB, True)
    og_s = og_s[:, :dec_seq]
    lf_row = jnp.pad(lf_s[:, :N_HEADS].reshape(dec_b, 1, dec_seq * N_HEADS),
                     ((0, 0), (0, 0), (0, PAGE_SIZE * N_HEADS - dec_seq * N_HEADS)))
    of_s = _fox_paged(page_table, dec3(fq_s), dec3(fk_s), dec3(fv_s), lf_row, cache_k, cache_v, cache_lf, l)
    oc_s, op_s = _convpool(dec3(u_s), dec3(pu_s), _pad_rows_front(state_conv[l], CONV_HALO),
                           _pad_rows_front(state_pool[l], POOL_MAX), lp, dec_seq, past, True)
    x1_s, h2_s, lg_s = _outproj(xs, flat(og_s), flat(of_s), flat(oc_s), flat(op_s), lp, ts, True)
    xs_new = _moe_dense(h2_s, lg_s, x1_s, lp, ts, True)
    heads = lambda a: a.reshape(dec_b, dec_seq, N_HEADS, HEAD_DIM)
    conv_s = jnp.concatenate([state_conv[l].astype(F32), dec3(u_s)], axis=1)[:, -(CONV_WIDTH - 1):]
    pool_s = jnp.concatenate([state_pool[l].astype(F32), dec3(pu_s)], axis=1)[:, -(POOL_MAX - 1):]
    outs = (heads(fk_s), heads(fv_s), dec3(lf_s)[:, :, :N_HEADS], _bd_to_state(st_s), conv_s, pool_s)
    return xs_new, outs


def kernel(x_prompt, x_sample, cache_k, cache_v, cache_logf, page_table, state_gla, state_conv, state_pool,
           norm1_g, w_in, gla_w_a2, gla_b_a, gla_norm_g, fox_b_f, fox_q_norm_g, fox_k_norm_g,
           conv_w, conv_b, conv_ln_g, conv_ln_b, pool_w, pool_scale, w_out, norm2_g,
           router_wg, router_bg, router_we, router_be, exp_w_gate, exp_w_up, exp_w_down):
    params = dict(norm1_g=norm1_g, w_in=w_in, gla_w_a2=gla_w_a2, gla_b_a=gla_b_a, gla_norm_g=gla_norm_g,
                  fox_b_f=fox_b_f, fox_q_norm_g=fox_q_norm_g, fox_k_norm_g=fox_k_norm_g, conv_w=conv_w,
                  conv_b=conv_b, conv_ln_g=conv_ln_g, conv_ln_b=conv_ln_b, pool_w=pool_w, pool_scale=pool_scale,
                  w_out=w_out, norm2_g=norm2_g, router_wg=router_wg, router_bg=router_bg, router_we=router_we,
                  router_be=router_be, exp_w_gate=exp_w_gate, exp_w_up=exp_w_up, exp_w_down=exp_w_down)
    depth = w_in.shape[0]
    bsz, seq, _ = x_prompt.shape
    dec_b, dec_seq, _ = x_sample.shape
    n_pool = cache_k.shape[1]
    caches = (cache_k.reshape(depth * n_pool, PAGE_SIZE, GROUP_WIDTH),
              cache_v.reshape(depth * n_pool, PAGE_SIZE, GROUP_WIDTH),
              cache_logf.astype(F32).reshape(depth * n_pool, PAGE_SIZE * N_HEADS))
    xp = x_prompt.reshape(bsz * seq, D_MODEL)
    xs = x_sample.reshape(dec_b * dec_seq, D_MODEL)
    per_layer = []
    for l in range(depth):
        lp = _prep_layer(l, params)
        xp, (kp, vp, lfp, glap, convp, poolp) = _layer_prompt(lp, xp, bsz, seq)
        xs, (ks, vs, lfs, glas, convs, pools) = _layer_sample(l, lp, xs, dec_b, dec_seq, page_table, caches,
                                                              state_gla, state_conv, state_pool)
        per_layer.append((kp, vp, lfp, ks, vs, lfs, glap, glas, convp, convs, poolp, pools))
    stacked = tuple(jnp.stack([per_layer[l][i] for l in range(depth)]) for i in range(12))
    return (xp.reshape(bsz, seq, D_MODEL), xs.reshape(dec_b, dec_seq, D_MODEL)) + stacked
```

```python
import functools

import jax
import jax.numpy as jnp
import numpy as np
from jax import lax
from jax.experimental import pallas as pl
from jax.experimental.pallas import tpu as pltpu

F32 = jnp.float32
BF16 = jnp.bfloat16
HIGHEST = lax.Precision.HIGHEST

D_MODEL = 1024
GROUP_WIDTH = 256
HEAD_DIM = 64
N_HEADS = GROUP_WIDTH // HEAD_DIM
GLA_RANK = 16
GLA_TAU = 16.0
GLA_SUB = 16
PAGE_SIZE = 128
CONV_WIDTH = 31
CONV_HALO = 32
POOL_WINDOWS = (2, 4, 8, 16)
POOL_MAX = 16
MOE_GROUPS = 4
EXPERTS_PER_GROUP = 4
N_EXPERTS = MOE_GROUPS * EXPERTS_PER_GROUP
D_EXPERT = 512
EPS = 1e-6
LOG2E = 1.4426950408889634
LANES = 128
ROUTER_LANE0 = MOE_GROUPS
VMEM_LIMIT = 56 * 1024 * 1024

C_GLA, C_FQ, C_FK, C_FV, C_CA, C_CB, C_PU, C_SMALL, C_END = 0, 1024, 1280, 1536, 1792, 2048, 2304, 2560, 2688
SMALL_FF0 = 0
SMALL_GA0 = N_HEADS


def _cparams(sem):
    return pltpu.CompilerParams(dimension_semantics=sem, vmem_limit_bytes=VMEM_LIMIT)


def _log_sigmoid(x):
    return jnp.minimum(x, 0.0) - jnp.log1p(jnp.exp(-jnp.abs(x)))


_NN = (((1,), (0,)), ((), ()))
_NT = (((1,), (1,)), ((), ()))
_TN = (((0,), (0,)), ((), ()))


def _mm(a, b, precise, dims=_NN):
    if precise:
        return lax.dot_general(a.astype(F32), b.astype(F32), dims, precision=HIGHEST, preferred_element_type=F32)
    return lax.dot_general(a.astype(BF16), b.astype(BF16), dims, preferred_element_type=F32)


def _head_rms(x, bd_mean, g):
    ms = jnp.dot(x * x, bd_mean, precision=HIGHEST, preferred_element_type=F32)
    return x * lax.rsqrt(ms + EPS) * g


def _inproj_kernel(x_ref, g1_ref, w_ref, wa2_ref, ba_ref, bf_ref, gq_ref, gk_ref, bd_ref,
                   gla_ref, la_ref, fq_ref, fk_ref, fv_ref, lf_ref, u_ref, pu_ref, h_scr, *, precise):
    x = x_ref[...]
    ms = jnp.mean(x * x, axis=-1, keepdims=True)
    h_scr[...] = (x * lax.rsqrt(ms + EPS) * g1_ref[...]).astype(h_scr.dtype)

    def proj(c0, width):
        return _mm(h_scr[...], w_ref[:, c0:c0 + width], precise)

    for j in range(4):
        gla_ref[:, j * GROUP_WIDTH:(j + 1) * GROUP_WIDTH] = proj(C_GLA + j * GROUP_WIDTH, GROUP_WIDTH)
    small = proj(C_SMALL, LANES)
    la_pre = jnp.dot(small, wa2_ref[...], precision=HIGHEST, preferred_element_type=F32) + ba_ref[...]
    la_ref[...] = _log_sigmoid(la_pre) * (1.0 / GLA_TAU)
    lf_ref[...] = _log_sigmoid(small + bf_ref[...])
    bd = bd_ref[...]
    fq_ref[...] = _head_rms(proj(C_FQ, GROUP_WIDTH), bd, gq_ref[...])
    fk_ref[...] = _head_rms(proj(C_FK, GROUP_WIDTH), bd, gk_ref[...])
    fv_ref[...] = proj(C_FV, GROUP_WIDTH)
    u_ref[...] = proj(C_CA, GROUP_WIDTH) * jax.nn.sigmoid(proj(C_CB, GROUP_WIDTH))
    pu_ref[...] = proj(C_PU, GROUP_WIDTH)


def _inproj(x2d, lp, tm, precise):
    t = x2d.shape[0]
    row = lambda w: pl.BlockSpec((tm, w), lambda i: (i, 0))
    full = lambda a: pl.BlockSpec(a.shape, lambda i: (0,) * a.ndim)
    w_in = lp['w_in_f32'] if precise else lp['w_in']
    consts = (lp['g1'], w_in, lp['wa2'], lp['ba'], lp['bf'], lp['gq'], lp['gk'], lp['bd_mean'])
    widths = (D_MODEL, GROUP_WIDTH, GROUP_WIDTH, GROUP_WIDTH, GROUP_WIDTH, LANES, GROUP_WIDTH, GROUP_WIDTH)
    return pl.pallas_call(
        functools.partial(_inproj_kernel, precise=precise),
        grid=(t // tm,),
        in_specs=[row(D_MODEL)] + [full(a) for a in consts],
        out_specs=[row(w) for w in widths],
        out_shape=[jax.ShapeDtypeStruct((t, w), F32) for w in widths],
        scratch_shapes=[pltpu.VMEM((tm, D_MODEL), w_in.dtype)],
        compiler_params=_cparams(("arbitrary",)),
        name="inproj",
    )(x2d, *consts)


def _gla_kernel(qkvg_ref, la_ref, st0_ref, gn_ref, bd_ref, bdm_ref, tri_ref, o_ref, stf_ref,
                st_scr, b_scr, o_scr, *, ch, precise):
    c = pl.program_id(1)

    @pl.when(c == 0)
    def _():
        st_scr[...] = st0_ref[0]

    b_scr[...] = jnp.dot(tri_ref[...], la_ref[0], precision=HIGHEST, preferred_element_type=F32)
    rows = lax.broadcasted_iota(jnp.int32, (GLA_SUB, GROUP_WIDTH), 0)
    bd = bd_ref[...]
    mm_dtype = F32 if precise else BF16

    def step(i, carry):
        r0 = pl.multiple_of(i * GLA_SUB, GLA_SUB)
        q = qkvg_ref[0, pl.ds(r0, GLA_SUB), 0:GROUP_WIDTH] * (HEAD_DIM ** -0.5)
        k = qkvg_ref[0, pl.ds(r0, GLA_SUB), GROUP_WIDTH:2 * GROUP_WIDTH]
        v = qkvg_ref[0, pl.ds(r0, GLA_SUB), 2 * GROUP_WIDTH:3 * GROUP_WIDTH]
        b = b_scr[pl.ds(r0, GLA_SUB), :]
        b_last = b[GLA_SUB - 1:GLA_SUB, :]
        st = st_scr[...]
        o_inter = _mm(q * jnp.exp(b), st, precise, _NT)
        dst = _mm(v, k * jnp.exp(b_last - b), precise, _TN)
        st_scr[...] = st * jnp.exp(b_last) + dst * bd
        ws = []
        for s in range(GLA_SUB):
            decay = jnp.where(rows >= s, jnp.exp(b - b[s:s + 1, :]), 0.0)
            ws.append((decay * (q * k[s:s + 1, :])).astype(mm_dtype))
        a_b = _mm(jnp.concatenate(ws, axis=0), bd, precise)
        o_diag = a_b[0:GLA_SUB] * v[0:1, :]
        for s in range(1, GLA_SUB):
            o_diag = o_diag + a_b[s * GLA_SUB:(s + 1) * GLA_SUB] * v[s:s + 1, :]
        o_scr[pl.ds(r0, GLA_SUB), :] = o_inter + o_diag
        return carry

    lax.fori_loop(0, ch // GLA_SUB, step, 0)
    gate = qkvg_ref[0, :, 3 * GROUP_WIDTH:4 * GROUP_WIDTH]
    o_ref[0] = _head_rms(o_scr[...], bdm_ref[...], gn_ref[...]) * (gate * jax.nn.sigmoid(gate))

    @pl.when(c == pl.num_programs(1) - 1)
    def _():
        stf_ref[0] = st_scr[...]


def _gla(qkvg, la, st0, lp, ch, precise):
    b, l, _ = qkvg.shape
    tri = jnp.asarray(np.kron(np.eye(ch // GLA_SUB), np.tril(np.ones((GLA_SUB, GLA_SUB)))), F32)
    full = lambda a: pl.BlockSpec(a.shape, lambda i, j: (0,) * a.ndim)
    consts = (lp['gla_gn'], lp['bd_ones'], lp['bd_mean'], tri)
    return pl.pallas_call(
        functools.partial(_gla_kernel, ch=ch, precise=precise),
        grid=(b, l // ch),
        in_specs=[pl.BlockSpec((1, ch, 4 * GROUP_WIDTH), lambda i, j: (i, j, 0)),
                  pl.BlockSpec((1, ch, GROUP_WIDTH), lambda i, j: (i, j, 0)),
                  pl.BlockSpec((1, GROUP_WIDTH, GROUP_WIDTH), lambda i, j: (i, 0, 0))] + [full(a) for a in consts],
        out_specs=[pl.BlockSpec((1, ch, GROUP_WIDTH), lambda i, j: (i, j, 0)),
                   pl.BlockSpec((1, GROUP_WIDTH, GROUP_WIDTH), lambda i, j: (i, 0, 0))],
        out_shape=[jax.ShapeDtypeStruct((b, l, GROUP_WIDTH), F32),
                   jax.ShapeDtypeStruct((b, GROUP_WIDTH, GROUP_WIDTH), F32)],
        scratch_shapes=[pltpu.VMEM((GROUP_WIDTH, GROUP_WIDTH), F32),
                        pltpu.VMEM((ch, GROUP_WIDTH), F32),
                        pltpu.VMEM((ch, GROUP_WIDTH), F32)],
        compiler_params=_cparams(("arbitrary", "arbitrary")),
        name="gla",
    )(qkvg, la, st0, *consts)


def _state_to_bd(s):
    b = s.shape[0]
    st = jnp.swapaxes(s.astype(F32), 2, 3)
    out = jnp.zeros((b, N_HEADS, HEAD_DIM, N_HEADS, HEAD_DIM), F32)
    for h in range(N_HEADS):
        out = out.at[:, h, :, h, :].set(st[:, h])
    return out.reshape(b, GROUP_WIDTH, GROUP_WIDTH)


def _bd_to_state(st):
    b = st.shape[0]
    s5 = st.reshape(b, N_HEADS, HEAD_DIM, N_HEADS, HEAD_DIM)
    diag = jnp.stack([s5[:, h, :, h, :] for h in range(N_HEADS)], axis=1)
    return jnp.swapaxes(diag, 2, 3)


def _cumsum_kernel(lf_ref, tri_ref, crep_ref, ct_ref, carry_scr):
    @pl.when(pl.program_id(1) == 0)
    def _():
        carry_scr[...] = jnp.zeros_like(carry_scr)

    c = jnp.dot(tri_ref[...], lf_ref[0], precision=HIGHEST, preferred_element_type=F32) + carry_scr[...]
    carry_scr[...] = c[c.shape[0] - 1:, :]
    c2 = c * LOG2E
    for h in range(N_HEADS):
        crep_ref[0, h] = jnp.broadcast_to(c2[:, h:h + 1], c2.shape)
    ct_ref[0] = c2.T[0:8, :]


def _fox_cumsum(lf, tm):
    b, l, _ = lf.shape
    tri = jnp.asarray(np.tril(np.ones((tm, tm))), F32)
    return pl.pallas_call(
        _cumsum_kernel,
        grid=(b, l // tm),
        in_specs=[pl.BlockSpec((1, tm, LANES), lambda i, j: (i, j, 0)),
                  pl.BlockSpec((tm, tm), lambda i, j: (0, 0))],
        out_specs=[pl.BlockSpec((1, N_HEADS, tm, LANES), lambda i, j: (i, 0, j, 0)),
                   pl.BlockSpec((1, 8, tm), lambda i, j: (i, 0, j))],
        out_shape=[jax.ShapeDtypeStruct((b, N_HEADS, l, LANES), F32), jax.ShapeDtypeStruct((b, 8, l), F32)],
        scratch_shapes=[pltpu.VMEM((1, LANES), F32)],
        compiler_params=_cparams(("arbitrary", "arbitrary")),
        name="fox_cumsum",
    )(lf, tri)


def _fox_prompt_kernel(qt_ref, k_ref, vt_ref, crep_ref, ct_ref, o_ref, qm_scr, m_scr, l_scr, acc_scr, *, tq):
    qi = pl.program_id(1)
    ki = pl.program_id(2)

    @pl.when(ki == 0)
    def _():
        qt = qt_ref[0] * (LOG2E * HEAD_DIM ** -0.5)
        row_head = lax.broadcasted_iota(jnp.int32, qt.shape, 0) // HEAD_DIM
        for h in range(N_HEADS):
            qm_scr[h] = jnp.where(row_head == h, qt, 0.0).astype(BF16)
        m_scr[...] = jnp.full_like(m_scr, -jnp.inf)
        l_scr[...] = jnp.zeros_like(l_scr)
        acc_scr[...] = jnp.zeros_like(acc_scr)

    def block(masked):
        k = k_ref[0].astype(BF16)
        for h in range(N_HEADS):
            rows = slice(h * HEAD_DIM, (h + 1) * HEAD_DIM)
            s = jnp.dot(k, qm_scr[h], preferred_element_type=F32)
            s = s - jnp.tile(crep_ref[0, h], (1, tq // LANES)) + ct_ref[0, h:h + 1, :]
            if masked:
                key = lax.broadcasted_iota(jnp.int32, (tq, tq), 0)
                qry = lax.broadcasted_iota(jnp.int32, (tq, tq), 1)
                s = jnp.where(key <= qry, s, -jnp.inf)
            m_prev = m_scr[h:h + 1, :]
            m_new = jnp.maximum(m_prev, jnp.max(s, axis=0, keepdims=True))
            alpha = jnp.exp2(m_prev - m_new)
            p = jnp.exp2(s - m_new)
            l_scr[h:h + 1, :] = alpha * l_scr[h:h + 1, :] + jnp.sum(p, axis=0, keepdims=True)
            acc_scr[rows, :] = alpha * acc_scr[rows, :] + jnp.dot(vt_ref[0, rows, :].astype(BF16), p.astype(BF16),
                                                                  preferred_element_type=F32)
            m_scr[h:h + 1, :] = m_new

    @pl.when(ki < qi)
    def _():
        block(False)

    @pl.when(ki == qi)
    def _():
        block(True)
        for h in range(N_HEADS):
            rows = slice(h * HEAD_DIM, (h + 1) * HEAD_DIM)
            acc_scr[rows, :] = acc_scr[rows, :] / l_scr[h:h + 1, :]
        o_ref[0] = acc_scr[...].T


def _fox_prompt(fq, fk, fv, crep, ct, tq):
    b, l, _ = fq.shape
    n = l // tq
    qt = jnp.swapaxes(fq, 1, 2)
    vt = jnp.swapaxes(fv, 1, 2)
    kv = lambda i, j, k: jnp.minimum(k, j)
    return pl.pallas_call(
        functools.partial(_fox_prompt_kernel, tq=tq),
        grid=(b, n, n),
        in_specs=[pl.BlockSpec((1, GROUP_WIDTH, tq), lambda i, j, k: (i, 0, j)),
                  pl.BlockSpec((1, tq, GROUP_WIDTH), lambda i, j, k: (i, kv(i, j, k), 0)),
                  pl.BlockSpec((1, GROUP_WIDTH, tq), lambda i, j, k: (i, 0, kv(i, j, k))),
                  pl.BlockSpec((1, N_HEADS, tq, LANES), lambda i, j, k: (i, 0, kv(i, j, k), 0)),
                  pl.BlockSpec((1, 8, tq), lambda i, j, k: (i, 0, j))],
        out_specs=pl.BlockSpec((1, tq, GROUP_WIDTH), lambda i, j, k: (i, j, 0)),
        out_shape=jax.ShapeDtypeStruct((b, l, GROUP_WIDTH), F32),
        scratch_shapes=[pltpu.VMEM((N_HEADS, GROUP_WIDTH, tq), BF16), pltpu.VMEM((8, tq), F32),
                        pltpu.VMEM((8, tq), F32), pltpu.VMEM((GROUP_WIDTH, tq), F32)],
        compiler_params=_cparams(("arbitrary", "arbitrary", "arbitrary")),
        name="fox_prompt",
    )(qt, fk, vt, crep, ct)


PAGES_PER_CHUNK = 16
CHUNK_ROWS = PAGES_PER_CHUNK * PAGE_SIZE


def _fox_paged_kernel(pt_ref, q_ref, kn_ref, vn_ref, lfn_ref, mc_ref, mtot_ref, trix_ref, hmask_ref,
                      ck_hbm, cv_hbm, clf_hbm, o_ref,
                      lf_scr, c_scr, kbuf, vbuf, kn_scr, vn_scr, m_scr, l_scr, acc_scr, sem_lf, sem_kv,
                      *, n_pages, page0):
    bi = pl.program_id(0)
    n_chunks = n_pages // PAGES_PER_CHUNK
    n_new = q_ref.shape[1]
    n_rows = N_HEADS * n_new

    def lf_copy(j):
        pg = pt_ref[bi, j] + page0
        return pltpu.make_async_copy(clf_hbm.at[pl.ds(pg, 1), :], lf_scr.at[pl.ds(j, 1), :], sem_lf.at[0])

    def kv_copies(chunk, slot, p):
        pg = pt_ref[bi, chunk * PAGES_PER_CHUNK + p] + page0
        rows = pl.ds(p * PAGE_SIZE, PAGE_SIZE)
        return (pltpu.make_async_copy(ck_hbm.at[pg], kbuf.at[slot, rows, :], sem_kv.at[0, slot]),
                pltpu.make_async_copy(cv_hbm.at[pg], vbuf.at[slot, rows, :], sem_kv.at[1, slot]))

    def start_chunk(chunk, slot):
        for p in range(PAGES_PER_CHUNK):
            for cp in kv_copies(chunk, slot, p):
                cp.start()

    def wait_chunk(chunk, slot):
        for p in range(PAGES_PER_CHUNK):
            for cp in kv_copies(chunk, slot, p):
                cp.wait()

    def lf_start(j, carry):
        lf_copy(j).start()
        return carry

    def lf_wait(j, carry):
        lf_copy(j).wait()
        return carry

    lax.fori_loop(0, n_pages, lf_start, 0)
    start_chunk(0, 0)

    lf_scr[n_pages:, :] = jnp.zeros((lf_scr.shape[0] - n_pages, lf_scr.shape[1]), F32)
    lf_scr[n_pages:n_pages + 1, :] = lfn_ref[0]
    kn_scr[...] = jnp.zeros_like(kn_scr)
    vn_scr[...] = jnp.zeros_like(vn_scr)
    kn_scr[0:n_new, :] = kn_ref[0]
    vn_scr[0:n_new, :] = vn_ref[0]
    m_scr[...] = jnp.full_like(m_scr, -jnp.inf)
    l_scr[...] = jnp.zeros_like(l_scr)
    acc_scr[...] = jnp.zeros_like(acc_scr)

    lax.fori_loop(0, n_pages, lf_wait, 0)
    lf = lf_scr[...]
    c_local = jnp.dot(lf, mc_ref[...], precision=HIGHEST, preferred_element_type=F32)
    tot = jnp.dot(lf, mtot_ref[...], precision=HIGHEST, preferred_element_type=F32)
    c2d = c_local + jnp.dot(trix_ref[...], tot, precision=HIGHEST, preferred_element_type=F32)
    for j in range(n_pages + 1):
        for h in range(N_HEADS):
            c_scr[h:h + 1, j * PAGE_SIZE:(j + 1) * PAGE_SIZE] = c2d[j:j + 1, h * PAGE_SIZE:(h + 1) * PAGE_SIZE]

    hmask = hmask_ref[...]
    q4 = q_ref[0] * (HEAD_DIM ** -0.5)
    qbd = jnp.concatenate([q4] * N_HEADS, axis=0) * hmask
    lane = lax.broadcasted_iota(jnp.int32, (n_rows, PAGE_SIZE), 1)
    tok = lax.broadcasted_iota(jnp.int32, (n_rows, PAGE_SIZE), 0) % n_new

    def c_rows(start, width):
        return jnp.concatenate([jnp.broadcast_to(c_scr[h:h + 1, pl.ds(start, width)], (n_new, width))
                                for h in range(N_HEADS)], axis=0)

    cq = jnp.sum(jnp.where(lane == tok, c_rows(n_pages * PAGE_SIZE, PAGE_SIZE), 0.0), axis=-1, keepdims=True)

    def split(a):
        hi = a.astype(BF16)
        return hi, (a - hi.astype(F32)).astype(BF16)

    def mm3(a, b, dims):
        a_hi, a_lo = split(a)
        b_hi, b_lo = split(b)
        n = a.shape[0]
        both = lax.dot_general(jnp.concatenate([a_hi, a_lo], axis=0), b_hi, dims, preferred_element_type=F32)
        return both[0:n] + both[n:2 * n] + lax.dot_general(a_hi, b_lo, dims, preferred_element_type=F32)

    def attend(k_rows, v_rows, c_keys, masked):
        s = mm3(qbd, k_rows, _NT) + cq - c_keys
        if masked:
            s = jnp.where(lane <= tok, s, -jnp.inf)
        m_prev = m_scr[...]
        m_new = jnp.maximum(m_prev, jnp.max(s, axis=-1, keepdims=True))
        alpha = jnp.exp(m_prev - m_new)
        p = jnp.exp(s - m_new)
        l_scr[...] = alpha * l_scr[...] + jnp.sum(p, axis=-1, keepdims=True)
        acc_scr[...] = alpha * acc_scr[...] + mm3(p, v_rows, _NN)
        m_scr[...] = m_new

    def chunk_body(chunk, carry):
        slot = chunk % 2
        wait_chunk(chunk, slot)

        @pl.when(chunk + 1 < n_chunks)
        def _():
            start_chunk(chunk + 1, 1 - slot)

        start = pl.multiple_of(chunk * CHUNK_ROWS, CHUNK_ROWS)
        attend(kbuf[slot], vbuf[slot], c_rows(start, CHUNK_ROWS), False)
        return carry

    lax.fori_loop(0, n_chunks, chunk_body, 0)
    attend(kn_scr[...], vn_scr[...], c_rows(n_pages * PAGE_SIZE, PAGE_SIZE), True)

    o_full = acc_scr[...] / l_scr[...] * hmask
    o = o_full[0:n_new]
    for h in range(1, N_HEADS):
        o = o + o_full[h * n_new:(h + 1) * n_new]
    o_ref[0] = o


def _fox_paged(page_table, fq, fk, fv, lf_new_row, cache_k, cache_v, cache_lf, layer):
    s, n_new, _ = fq.shape
    n_pages = page_table.shape[1]
    n_pool = cache_k.shape[0] // 2
    n_rows = N_HEADS * n_new
    c_rows = ((n_pages + 1 + 7) // 8) * 8
    lanes = PAGE_SIZE * N_HEADS
    src = np.arange(lanes)
    dst = np.arange(lanes)
    same_head = (src[:, None] % N_HEADS) == (dst[None, :] // PAGE_SIZE)
    mc = jnp.asarray(same_head & ((src[:, None] // N_HEADS) <= (dst[None, :] % PAGE_SIZE)), F32)
    mtot = jnp.asarray(same_head, F32)
    trix = jnp.asarray(np.tril(np.ones((c_rows, c_rows)), -1), F32)
    hmask = jnp.asarray((np.arange(n_rows)[:, None] // n_new) == (np.arange(GROUP_WIDTH)[None, :] // HEAD_DIM), F32)
    tokspec = lambda w: pl.BlockSpec((1, n_new, w), lambda i, pt: (i, 0, 0))
    full = lambda a: pl.BlockSpec(a.shape, lambda i, pt: (0,) * a.ndim)
    anyspec = pl.BlockSpec(memory_space=pl.ANY)
    grid_spec = pltpu.PrefetchScalarGridSpec(
        num_scalar_prefetch=1,
        grid=(s,),
        in_specs=[tokspec(GROUP_WIDTH), tokspec(GROUP_WIDTH), tokspec(GROUP_WIDTH),
                  pl.BlockSpec((1, 1, lanes), lambda i, pt: (i, 0, 0)),
                  full(mc), full(mtot), full(trix), full(hmask), anyspec, anyspec, anyspec],
        out_specs=tokspec(GROUP_WIDTH),
        scratch_shapes=[pltpu.VMEM((c_rows, lanes), F32), pltpu.VMEM((8, (n_pages + 1) * PAGE_SIZE), F32),
                        pltpu.VMEM((2, CHUNK_ROWS, GROUP_WIDTH), F32),
                        pltpu.VMEM((2, CHUNK_ROWS, GROUP_WIDTH), F32),
                        pltpu.VMEM((PAGE_SIZE, GROUP_WIDTH), F32), pltpu.VMEM((PAGE_SIZE, GROUP_WIDTH), F32),
                        pltpu.VMEM((n_rows, 1), F32), pltpu.VMEM((n_rows, 1), F32),
                        pltpu.VMEM((n_rows, GROUP_WIDTH), F32),
                        pltpu.SemaphoreType.DMA((1,)), pltpu.SemaphoreType.DMA((2, 2))],
    )
    return pl.pallas_call(
        functools.partial(_fox_paged_kernel, n_pages=n_pages, page0=layer * n_pool),
        grid_spec=grid_spec,
        out_shape=jax.ShapeDtypeStruct((s, n_new, GROUP_WIDTH), F32),
        compiler_params=_cparams(("arbitrary",)),
        name="fox_paged",
    )(page_table, fq, fk, fv, lf_new_row, mc, mtot, trix, hmask, cache_k, cache_v, cache_lf)


def _convpool_kernel(u_ref, pu_ref, cpre_ref, ppre_ref, cw_ref, cb_ref, lng_ref, lnb_ref, pw_ref, ps_ref,
                     oc_ref, op_ref, cext, pext, *, tm, pos0, precise):
    i = pl.program_id(1)

    @pl.when(i == 0)
    def _():
        cext[0:CONV_HALO, :] = cpre_ref[0]
        pext[0:POOL_MAX, :] = ppre_ref[0]

    @pl.when(i > 0)
    def _():
        cext[0:CONV_HALO, :] = cext[tm:tm + CONV_HALO, :]
        pext[0:POOL_MAX, :] = pext[tm:tm + POOL_MAX, :]

    cext[CONV_HALO:CONV_HALO + tm, :] = u_ref[0]
    pext[POOL_MAX:POOL_MAX + tm, :] = pu_ref[0]

    off = CONV_HALO - (CONV_WIDTH - 1)
    y = cext[off:off + tm, :] * cw_ref[0:1, :]
    for j in range(1, CONV_WIDTH):
        y = y + cext[off + j:off + j + tm, :] * cw_ref[j:j + 1, :]
    y = y + cb_ref[...]
    mu = jnp.mean(y, axis=-1, keepdims=True)
    yc = y - mu
    var = jnp.mean(yc * yc, axis=-1, keepdims=True)
    yn = yc * lax.rsqrt(var + EPS) * lng_ref[...] + lnb_ref[...]
    oc_ref[0] = yn * jax.nn.sigmoid(yn)

    lane_group = lax.broadcasted_iota(jnp.int32, (tm, GROUP_WIDTH), 1) // (GROUP_WIDTH // len(POOL_WINDOWS))
    pos = pos0 + i * tm + lax.broadcasted_iota(jnp.int32, (tm, 1), 0)
    x0 = pext[POOL_MAX:POOL_MAX + tm, :]
    run = x0
    mean = jnp.zeros((tm, GROUP_WIDTH), F32)
    for back in range(1, POOL_MAX):
        run = run + pext[POOL_MAX - back:POOL_MAX - back + tm, :]
        if back + 1 in POOL_WINDOWS:
            g = POOL_WINDOWS.index(back + 1)
            cnt = jnp.minimum(back + 1, pos + 1).astype(F32)
            mean = jnp.where(lane_group == g, run / cnt, mean)
    d = mean - x0
    op_ref[0] = _mm(d, pw_ref[...], precise) * ps_ref[...]


def _convpool(u, pu, conv_prefix, pool_prefix, lp, tm, pos0, precise):
    b, l, _ = u.shape
    seq = pl.BlockSpec((1, tm, GROUP_WIDTH), lambda i, j: (i, j, 0))
    full = lambda a: pl.BlockSpec(a.shape, lambda i, j: (0,) * a.ndim)
    consts = (lp['conv_w'], lp['conv_b'], lp['conv_ln_g'], lp['conv_ln_b'],
              lp['pool_w_f32'] if precise else lp['pool_w'], lp['pool_scale'])
    return pl.pallas_call(
        functools.partial(_convpool_kernel, tm=tm, pos0=pos0, precise=precise),
        grid=(b, l // tm),
        in_specs=[seq, seq,
                  pl.BlockSpec((1, CONV_HALO, GROUP_WIDTH), lambda i, j: (i, 0, 0)),
                  pl.BlockSpec((1, POOL_MAX, GROUP_WIDTH), lambda i, j: (i, 0, 0))] + [full(a) for a in consts],
        out_specs=[seq, seq],
        out_shape=[jax.ShapeDtypeStruct((b, l, GROUP_WIDTH), F32)] * 2,
        scratch_shapes=[pltpu.VMEM((CONV_HALO + tm, GROUP_WIDTH), F32), pltpu.VMEM((POOL_MAX + tm, GROUP_WIDTH), F32)],
        compiler_params=_cparams(("arbitrary", "arbitrary")),
        name="convpool",
    )(u, pu, conv_prefix, pool_prefix, *consts)


def _router_logits(h2, wr):
    return jnp.dot(h2, wr, precision=HIGHEST, preferred_element_type=F32)


def _outproj_kernel(x_ref, og_ref, of_ref, oc_ref, op_ref, w_ref, g2_ref, wr_ref, br_ref,
                    x1_ref, h2_ref, lg_ref, *, precise):
    acc = x_ref[...]
    for j, m_ref in enumerate((og_ref, of_ref, oc_ref, op_ref)):
        acc = acc + _mm(m_ref[...], w_ref[j * GROUP_WIDTH:(j + 1) * GROUP_WIDTH, :], precise)
    x1_ref[...] = acc
    ms = jnp.mean(acc * acc, axis=-1, keepdims=True)
    h2 = acc * lax.rsqrt(ms + EPS) * g2_ref[...]
    h2_ref[...] = h2.astype(h2_ref.dtype)
    lg_ref[...] = _router_logits(h2, wr_ref[...]) + br_ref[...]


def _outproj(x2d, og, of, oc, op, lp, tm, precise):
    t = x2d.shape[0]
    row = lambda w: pl.BlockSpec((tm, w), lambda i: (i, 0))
    full = lambda a: pl.BlockSpec(a.shape, lambda i: (0,) * a.ndim)
    consts = (lp['w_out_f32'] if precise else lp['w_out'], lp['g2'], lp['w_router'], lp['b_router'])
    return pl.pallas_call(
        functools.partial(_outproj_kernel, precise=precise),
        grid=(t // tm,),
        in_specs=[row(D_MODEL)] + [row(GROUP_WIDTH)] * 4 + [full(a) for a in consts],
        out_specs=[row(D_MODEL), row(D_MODEL), row(LANES)],
        out_shape=[jax.ShapeDtypeStruct((t, D_MODEL), F32),
                   jax.ShapeDtypeStruct((t, D_MODEL), F32 if precise else BF16),
                   jax.ShapeDtypeStruct((t, LANES), F32)],
        compiler_params=_cparams(("arbitrary",)),
        name="outproj",
    )(x2d, og, of, oc, op, *consts)


def _route(lg):
    lane = lax.broadcasted_iota(jnp.int32, lg.shape, 1)
    big = jnp.int32(LANES)
    neg = -jnp.inf
    is_g = lane < MOE_GROUPS
    gl = jnp.where(is_g, lg, neg)
    gmax = jnp.max(gl, axis=-1, keepdims=True)
    gidx = jnp.min(jnp.where(gl == gmax, lane, big), axis=-1, keepdims=True)
    p_group = 1.0 / jnp.sum(jnp.where(is_g, jnp.exp(lg - gmax), 0.0), axis=-1, keepdims=True)
    lo = ROUTER_LANE0 + gidx * EXPERTS_PER_GROUP
    el = jnp.where((lane >= lo) & (lane < lo + EXPERTS_PER_GROUP), lg, neg)
    v1 = jnp.max(el, axis=-1, keepdims=True)
    i1 = jnp.min(jnp.where(el == v1, lane, big), axis=-1, keepdims=True)
    el2 = jnp.where(lane == i1, neg, el)
    v2 = jnp.max(el2, axis=-1, keepdims=True)
    i2 = jnp.min(jnp.where(el2 == v2, lane, big), axis=-1, keepdims=True)
    e21 = jnp.exp(v2 - v1)
    g1 = 1.0 / (1.0 + e21)
    g2 = e21 * g1
    return jnp.where(lane == i1, g1 * p_group, 0.0) + jnp.where(lane == i2, g2 * p_group, 0.0)


def _moe_dense_kernel(h_ref, lg_ref, x1_ref, wg_ref, wu_ref, wd_ref, o_ref, comb_scr, acc_scr, *, precise):
    e = pl.program_id(1)

    @pl.when(e == 0)
    def _():
        comb_scr[...] = _route(lg_ref[...])
        acc_scr[...] = x1_ref[...]

    h = h_ref[...]
    a = _mm(h, wg_ref[0], precise)
    b = _mm(h, wu_ref[0], precise)
    y = _mm(a * jax.nn.sigmoid(a) * b, wd_ref[0], precise)
    lane = lax.broadcasted_iota(jnp.int32, comb_scr.shape, 1)
    cw = jnp.sum(jnp.where(lane == e + ROUTER_LANE0, comb_scr[...], 0.0), axis=-1, keepdims=True)
    acc_scr[...] = acc_scr[...] + cw * y

    @pl.when(e == N_EXPERTS - 1)
    def _():
        o_ref[...] = acc_scr[...]


def _moe_dense(h2, lg, x1, lp, tm, precise):
    t = h2.shape[0]
    row = lambda w: pl.BlockSpec((tm, w), lambda i, e: (i, 0))
    sfx = '_f32' if precise else ''
    return pl.pallas_call(
        functools.partial(_moe_dense_kernel, precise=precise),
        grid=(t // tm, N_EXPERTS),
        in_specs=[row(D_MODEL), row(LANES), row(D_MODEL),
                  pl.BlockSpec((1, D_MODEL, D_EXPERT), lambda i, e: (e, 0, 0)),
                  pl.BlockSpec((1, D_MODEL, D_EXPERT), lambda i, e: (e, 0, 0)),
                  pl.BlockSpec((1, D_EXPERT, D_MODEL), lambda i, e: (e, 0, 0))],
        out_specs=row(D_MODEL),
        out_shape=jax.ShapeDtypeStruct((t, D_MODEL), F32),
        scratch_shapes=[pltpu.VMEM((tm, LANES), F32), pltpu.VMEM((tm, D_MODEL), F32)],
        compiler_params=_cparams(("arbitrary", "arbitrary")),
        name="moe_dense",
    )(h2, lg, x1, lp['exp_w_gate' + sfx], lp['exp_w_up' + sfx], lp['exp_w_down' + sfx])


def _block_diag(blocks):
    n, r, c = blocks.shape
    out = jnp.zeros((n, r, n, c), blocks.dtype)
    for g in range(n):
        out = out.at[g, :, g, :].set(blocks[g])
    return out.reshape(n * r, n * c)


def _prep_layer(l, p):
    w = p['w_in'][l]
    o = np.cumsum((0, 256, 256, 256, 256, GLA_RANK, 256, 256, 256, N_HEADS, 256, 256, 256))
    small = jnp.concatenate([w[:, o[8]:o[9]], w[:, o[4]:o[5]],
                             jnp.zeros((D_MODEL, LANES - N_HEADS - GLA_RANK), F32)], axis=1)
    w_packed = jnp.concatenate([w[:, o[0]:o[4]], w[:, o[5]:o[8]], w[:, o[9]:o[12]], small], axis=1)
    wa2 = jnp.zeros((LANES, GROUP_WIDTH), F32).at[SMALL_GA0:SMALL_GA0 + GLA_RANK].set(p['gla_w_a2'][l])
    bf = jnp.zeros((1, LANES), F32).at[0, SMALL_FF0:SMALL_FF0 + N_HEADS].set(p['fox_b_f'][l])
    ones_bd = _block_diag(jnp.ones((N_HEADS, HEAD_DIM, HEAD_DIM), F32))
    w_router = jnp.concatenate([p['router_wg'][l], p['router_we'][l],
                                jnp.zeros((D_MODEL, LANES - MOE_GROUPS - N_EXPERTS), F32)], axis=1)
    b_router = jnp.concatenate([p['router_bg'][l], p['router_be'][l],
                                jnp.zeros((LANES - MOE_GROUPS - N_EXPERTS,), F32)])[None, :]
    conv_w = jnp.concatenate([p['conv_w'][l], jnp.zeros((CONV_HALO - CONV_WIDTH, GROUP_WIDTH), F32)], axis=0)
    row = lambda a: a.reshape(1, -1).astype(F32)
    return {
        'g1': row(p['norm1_g'][l]), 'w_in': w_packed.astype(BF16), 'wa2': wa2, 'ba': row(p['gla_b_a'][l]), 'bf': bf,
        'gq': row(jnp.tile(p['fox_q_norm_g'][l], N_HEADS)), 'gk': row(jnp.tile(p['fox_k_norm_g'][l], N_HEADS)),
        'bd_mean': ones_bd * (1.0 / HEAD_DIM), 'bd_ones': ones_bd, 'gla_gn': row(p['gla_norm_g'][l]),
        'conv_w': conv_w, 'conv_b': row(p['conv_b'][l]), 'conv_ln_g': row(p['conv_ln_g'][l]),
        'conv_ln_b': row(p['conv_ln_b'][l]), 'pool_w': _block_diag(p['pool_w'][l]).astype(BF16),
        'pool_scale': row(p['pool_scale'][l]), 'w_out': p['w_out'][l].astype(BF16), 'g2': row(p['norm2_g'][l]),
        'w_router': w_router, 'b_router': b_router,
        'exp_w_gate': p['exp_w_gate'][l].astype(BF16), 'exp_w_up': p['exp_w_up'][l].astype(BF16),
        'exp_w_down': p['exp_w_down'][l].astype(BF16),
        'w_in_f32': w_packed, 'pool_w_f32': _block_diag(p['pool_w'][l].astype(F32)), 'w_out_f32': p['w_out'][l],
        'exp_w_gate_f32': p['exp_w_gate'][l], 'exp_w_up_f32': p['exp_w_up'][l], 'exp_w_down_f32': p['exp_w_down'][l],
    }


def _pad_rows_front(a, rows):
    return jnp.pad(a.astype(F32), ((0, 0), (rows - a.shape[1], 0), (0, 0)))


def _flat(a):
    return a.reshape(-1, a.shape[-1])


def _layer_prompt(lp, xp, bsz, seq):
    tp = 512
    gla_p, la_p, fq_p, fk_p, fv_p, lf_p, u_p, pu_p = _inproj(xp, lp, tp, False)
    seq3 = lambda a: a.reshape(bsz, seq, a.shape[-1])
    og_p, st_p = _gla(seq3(gla_p), seq3(la_p), jnp.zeros((bsz, GROUP_WIDTH, GROUP_WIDTH), F32), lp, 128, False)
    c_p, ct_p = _fox_cumsum(seq3(lf_p), 512)
    of_p = _fox_prompt(seq3(fq_p), seq3(fk_p), seq3(fv_p), c_p, ct_p, 512)
    oc_p, op_p = _convpool(seq3(u_p), seq3(pu_p), jnp.zeros((bsz, CONV_HALO, GROUP_WIDTH), F32),
                           jnp.zeros((bsz, POOL_MAX, GROUP_WIDTH), F32), lp, 512, 0, False)
    x1_p, h2_p, lg_p = _outproj(xp, _flat(og_p), _flat(of_p), _flat(oc_p), _flat(op_p), lp, tp, False)
    xp_new = _moe_dense(h2_p, lg_p, x1_p, lp, 1024, False)
    heads = lambda a: a.reshape(bsz, seq, N_HEADS, HEAD_DIM)
    outs = (heads(fk_p), heads(fv_p), seq3(lf_p)[:, :, :N_HEADS], _bd_to_state(st_p),
            seq3(u_p)[:, -(CONV_WIDTH - 1):], seq3(pu_p)[:, -(POOL_MAX - 1):])
    return xp_new, outs


def _layer_sample(l, lp, xs, dec_b, dec_seq, page_table, caches, state_gla, state_conv, state_pool):
    cache_k, cache_v, cache_lf = caches
    past = page_table.shape[1] * PAGE_SIZE
    flat = _flat
    ts = xs.shape[0]
    gla_s, la_s, fq_s, fk_s, fv_s, lf_s, u_s, pu_s = _inproj(xs, lp, ts, True)
    dec3 = lambda a: a.reshape(dec_b, dec_seq, a.shape[-1])
    pad_t = lambda a: jnp.pad(dec3(a), ((0, 0), (0, GLA_SUB - dec_seq), (0, 0)))
    og_s, st_s = _gla(pad_t(gla_s), pad_t(la_s), _state_to_bd(state_gla[l]), lp, GLA_SUB, True)
    og_s = og_s[:, :dec_seq]
    lf_row = jnp.pad(lf_s[:, :N_HEADS].reshape(dec_b, 1, dec_seq * N_HEADS),
                     ((0, 0), (0, 0), (0, PAGE_SIZE * N_HEADS - dec_seq * N_HEADS)))
    of_s = _fox_paged(page_table, dec3(fq_s), dec3(fk_s), dec3(fv_s), lf_row, cache_k, cache_v, cache_lf, l)
    oc_s, op_s = _convpool(dec3(u_s), dec3(pu_s), _pad_rows_front(state_conv[l], CONV_HALO),
                           _pad_rows_front(state_pool[l], POOL_MAX), lp, dec_seq, past, True)
    x1_s, h2_s, lg_s = _outproj(xs, flat(og_s), flat(of_s), flat(oc_s), flat(op_s), lp, ts, True)
    xs_new = _moe_dense(h2_s, lg_s, x1_s, lp, ts, True)
    heads = lambda a: a.reshape(dec_b, dec_seq, N_HEADS, HEAD_DIM)
    conv_s = jnp.concatenate([state_conv[l].astype(F32), dec3(u_s)], axis=1)[:, -(CONV_WIDTH - 1):]
    pool_s = jnp.concatenate([state_pool[l].astype(F32), dec3(pu_s)], axis=1)[:, -(POOL_MAX - 1):]
    outs = (heads(fk_s), heads(fv_s), dec3(lf_s)[:, :, :N_HEADS], _bd_to_state(st_s), conv_s, pool_s)
    return xs_new, outs


def kernel(x_prompt, x_sample, cache_k, cache_v, cache_logf, page_table, state_gla, state_conv, state_pool,
           norm1_g, w_in, gla_w_a2, gla_b_a, gla_norm_g, fox_b_f, fox_q_norm_g, fox_k_norm_g,
           conv_w, conv_b, conv_ln_g, conv_ln_b, pool_w, pool_scale, w_out, norm2_g,
           router_wg, router_bg, router_we, router_be, exp_w_gate, exp_w_up, exp_w_down):
    params = dict(norm1_g=norm1_g, w_in=w_in, gla_w_a2=gla_w_a2, gla_b_a=gla_b_a, gla_norm_g=gla_norm_g,
                  fox_b_f=fox_b_f, fox_q_norm_g=fox_q_norm_g, fox_k_norm_g=fox_k_norm_g, conv_w=conv_w,
                  conv_b=conv_b, conv_ln_g=conv_ln_g, conv_ln_b=conv_ln_b, pool_w=pool_w, pool_scale=pool_scale,
                  w_out=w_out, norm2_g=norm2_g, router_wg=router_wg, router_bg=router_bg, router_we=router_we,
                  router_be=router_be, exp_w_gate=exp_w_gate, exp_w_up=exp_w_up, exp_w_down=exp_w_down)
    depth = w_in.shape[0]
    bsz, seq, _ = x_prompt.shape
    dec_b, dec_seq, _ = x_sample.shape
    n_pool = cache_k.shape[1]
    caches = (cache_k.reshape(depth * n_pool, PAGE_SIZE, GROUP_WIDTH),
              cache_v.reshape(depth * n_pool, PAGE_SIZE, GROUP_WIDTH),
              cache_logf.astype(F32).reshape(depth * n_pool, PAGE_SIZE * N_HEADS))
    xp = x_prompt.reshape(bsz * seq, D_MODEL)
    xs = x_sample.reshape(dec_b * dec_seq, D_MODEL)
    per_layer = []
    for l in range(depth):
        lp = _prep_layer(l, params)
        xp, (kp, vp, lfp, glap, convp, poolp) = _layer_prompt(lp, xp, bsz, seq)
        xs, (ks, vs, lfs, glas, convs, pools) = _layer_sample(l, lp, xs, dec_b, dec_seq, page_table, caches,
                                                              state_gla, state_conv, state_pool)
        per_layer.append((kp, vp, lfp, ks, vs, lfs, glap, glas, convp, convs, poolp, pools))
    stacked = tuple(jnp.stack([per_layer[l][i] for l in range(depth)]) for i in range(12))
    return (xp.reshape(bsz, seq, D_MODEL), xs.reshape(dec_b, dec_seq, D_MODEL)) + stacked
```

```python
import functools

import jax
import jax.numpy as jnp
import numpy as np
from jax import lax
from jax.experimental import pallas as pl
from jax.experimental.pallas import tpu as pltpu

F32 = jnp.float32
BF16 = jnp.bfloat16
HIGHEST = lax.Precision.HIGHEST

D_MODEL = 1024
GROUP_WIDTH = 256
HEAD_DIM = 64
N_HEADS = GROUP_WIDTH // HEAD_DIM
GLA_RANK = 16
GLA_TAU = 16.0
GLA_SUB = 16
PAGE_SIZE = 128
CONV_WIDTH = 31
CONV_HALO = 32
POOL_WINDOWS = (2, 4, 8, 16)
POOL_MAX = 16
MOE_GROUPS = 4
EXPERTS_PER_GROUP = 4
N_EXPERTS = MOE_GROUPS * EXPERTS_PER_GROUP
D_EXPERT = 512
EPS = 1e-6
LOG2E = 1.4426950408889634
LANES = 128
ROUTER_LANE0 = MOE_GROUPS
VMEM_LIMIT = 56 * 1024 * 1024

C_GLA, C_FQ, C_FK, C_FV, C_CA, C_CB, C_PU, C_SMALL, C_END = 0, 1024, 1280, 1536, 1792, 2048, 2304, 2560, 2688
SMALL_FF0 = 0
SMALL_GA0 = N_HEADS


def _cparams(sem):
    return pltpu.CompilerParams(dimension_semantics=sem, vmem_limit_bytes=VMEM_LIMIT)


def _log_sigmoid(x):
    return jnp.minimum(x, 0.0) - jnp.log1p(jnp.exp(-jnp.abs(x)))


_NN = (((1,), (0,)), ((), ()))
_NT = (((1,), (1,)), ((), ()))
_TN = (((0,), (0,)), ((), ()))


def _mm(a, b, precise, dims=_NN):
    if precise:
        return lax.dot_general(a.astype(F32), b.astype(F32), dims, precision=HIGHEST, preferred_element_type=F32)
    return lax.dot_general(a.astype(BF16), b.astype(BF16), dims, preferred_element_type=F32)


def _head_rms(x, bd_mean, g):
    ms = jnp.dot(x * x, bd_mean, precision=HIGHEST, preferred_element_type=F32)
    return x * lax.rsqrt(ms + EPS) * g


def _inproj_kernel(x_ref, g1_ref, w_ref, wa2_ref, ba_ref, bf_ref, gq_ref, gk_ref, bd_ref,
                   gla_ref, la_ref, fq_ref, fk_ref, fv_ref, lf_ref, u_ref, pu_ref, h_scr, *, precise):
    x = x_ref[...]
    ms = jnp.mean(x * x, axis=-1, keepdims=True)
    h_scr[...] = (x * lax.rsqrt(ms + EPS) * g1_ref[...]).astype(h_scr.dtype)

    def proj(c0, width):
        return _mm(h_scr[...], w_ref[:, c0:c0 + width], precise)

    for j in range(4):
        gla_ref[:, j * GROUP_WIDTH:(j + 1) * GROUP_WIDTH] = proj(C_GLA + j * GROUP_WIDTH, GROUP_WIDTH)
    small = proj(C_SMALL, LANES)
    la_pre = jnp.dot(small, wa2_ref[...], precision=HIGHEST, preferred_element_type=F32) + ba_ref[...]
    la_ref[...] = _log_sigmoid(la_pre) * (1.0 / GLA_TAU)
    lf_ref[...] = _log_sigmoid(small + bf_ref[...])
    bd = bd_ref[...]
    fq_ref[...] = _head_rms(proj(C_FQ, GROUP_WIDTH), bd, gq_ref[...])
    fk_ref[...] = _head_rms(proj(C_FK, GROUP_WIDTH), bd, gk_ref[...])
    fv_ref[...] = proj(C_FV, GROUP_WIDTH)
    u_ref[...] = proj(C_CA, GROUP_WIDTH) * jax.nn.sigmoid(proj(C_CB, GROUP_WIDTH))
    pu_ref[...] = proj(C_PU, GROUP_WIDTH)


def _inproj(x2d, lp, tm, precise):
    t = x2d.shape[0]
    row = lambda w: pl.BlockSpec((tm, w), lambda i: (i, 0))
    full = lambda a: pl.BlockSpec(a.shape, lambda i: (0,) * a.ndim)
    w_in = lp['w_in_f32'] if precise else lp['w_in']
    consts = (lp['g1'], w_in, lp['wa2'], lp['ba'], lp['bf'], lp['gq'], lp['gk'], lp['bd_mean'])
    widths = (D_MODEL, GROUP_WIDTH, GROUP_WIDTH, GROUP_WIDTH, GROUP_WIDTH, LANES, GROUP_WIDTH, GROUP_WIDTH)
    return pl.pallas_call(
        functools.partial(_inproj_kernel, precise=precise),
        grid=(t // tm,),
        in_specs=[row(D_MODEL)] + [full(a) for a in consts],
        out_specs=[row(w) for w in widths],
        out_shape=[jax.ShapeDtypeStruct((t, w), F32) for w in widths],
        scratch_shapes=[pltpu.VMEM((tm, D_MODEL), w_in.dtype)],
        compiler_params=_cparams(("arbitrary",)),
        name="inproj",
    )(x2d, *consts)


def _gla_kernel(qkvg_ref, la_ref, st0_ref, gn_ref, bd_ref, bdm_ref, tri_ref, o_ref, stf_ref,
                st_scr, b_scr, o_scr, *, ch, precise):
    c = pl.program_id(1)

    @pl.when(c == 0)
    def _():
        st_scr[...] = st0_ref[0]

    b_scr[...] = jnp.dot(tri_ref[...], la_ref[0], precision=HIGHEST, preferred_element_type=F32)
    rows = lax.broadcasted_iota(jnp.int32, (GLA_SUB, GROUP_WIDTH), 0)
    bd = bd_ref[...]
    mm_dtype = F32 if precise else BF16

    def step(i, carry):
        r0 = pl.multiple_of(i * GLA_SUB, GLA_SUB)
        q = qkvg_ref[0, pl.ds(r0, GLA_SUB), 0:GROUP_WIDTH] * (HEAD_DIM ** -0.5)
        k = qkvg_ref[0, pl.ds(r0, GLA_SUB), GROUP_WIDTH:2 * GROUP_WIDTH]
        v = qkvg_ref[0, pl.ds(r0, GLA_SUB), 2 * GROUP_WIDTH:3 * GROUP_WIDTH]
        b = b_scr[pl.ds(r0, GLA_SUB), :]
        b_last = b[GLA_SUB - 1:GLA_SUB, :]
        st = st_scr[...]
        o_inter = _mm(q * jnp.exp(b), st, precise, _NT)
        dst = _mm(v, k * jnp.exp(b_last - b), precise, _TN)
        st_scr[...] = st * jnp.exp(b_last) + dst * bd
        ws = []
        for s in range(GLA_SUB):
            decay = jnp.where(rows >= s, jnp.exp(b - b[s:s + 1, :]), 0.0)
            ws.append((decay * (q * k[s:s + 1, :])).astype(mm_dtype))
        a_b = _mm(jnp.concatenate(ws, axis=0), bd, precise)
        o_diag = a_b[0:GLA_SUB] * v[0:1, :]
        for s in range(1, GLA_SUB):
            o_diag = o_diag + a_b[s * GLA_SUB:(s + 1) * GLA_SUB] * v[s:s + 1, :]
        o_scr[pl.ds(r0, GLA_SUB), :] = o_inter + o_diag
        return carry

    lax.fori_loop(0, ch // GLA_SUB, step, 0)
    gate = qkvg_ref[0, :, 3 * GROUP_WIDTH:4 * GROUP_WIDTH]
    o_ref[0] = _head_rms(o_scr[...], bdm_ref[...], gn_ref[...]) * (gate * jax.nn.sigmoid(gate))

    @pl.when(c == pl.num_programs(1) - 1)
    def _():
        stf_ref[0] = st_scr[...]


def _gla(qkvg, la, st0, lp, ch, precise):
    b, l, _ = qkvg.shape
    tri = jnp.asarray(np.kron(np.eye(ch // GLA_SUB), np.tril(np.ones((GLA_SUB, GLA_SUB)))), F32)
    full = lambda a: pl.BlockSpec(a.shape, lambda i, j: (0,) * a.ndim)
    consts = (lp['gla_gn'], lp['bd_ones'], lp['bd_mean'], tri)
    return pl.pallas_call(
        functools.partial(_gla_kernel, ch=ch, precise=precise),
        grid=(b, l // ch),
        in_specs=[pl.BlockSpec((1, ch, 4 * GROUP_WIDTH), lambda i, j: (i, j, 0)),
                  pl.BlockSpec((1, ch, GROUP_WIDTH), lambda i, j: (i, j, 0)),
                  pl.BlockSpec((1, GROUP_WIDTH, GROUP_WIDTH), lambda i, j: (i, 0, 0))] + [full(a) for a in consts],
        out_specs=[pl.BlockSpec((1, ch, GROUP_WIDTH), lambda i, j: (i, j, 0)),
                   pl.BlockSpec((1, GROUP_WIDTH, GROUP_WIDTH), lambda i, j: (i, 0, 0))],
        out_shape=[jax.ShapeDtypeStruct((b, l, GROUP_WIDTH), F32),
                   jax.ShapeDtypeStruct((b, GROUP_WIDTH, GROUP_WIDTH), F32)],
        scratch_shapes=[pltpu.VMEM((GROUP_WIDTH, GROUP_WIDTH), F32),
                        pltpu.VMEM((ch, GROUP_WIDTH), F32),
                        pltpu.VMEM((ch, GROUP_WIDTH), F32)],
        compiler_params=_cparams(("arbitrary", "arbitrary")),
        name="gla",
    )(qkvg, la, st0, *consts)


def _state_to_bd(s):
    b = s.shape[0]
    st = jnp.swapaxes(s.astype(F32), 2, 3)
    out = jnp.zeros((b, N_HEADS, HEAD_DIM, N_HEADS, HEAD_DIM), F32)
    for h in range(N_HEADS):
        out = out.at[:, h, :, h, :].set(st[:, h])
    return out.reshape(b, GROUP_WIDTH, GROUP_WIDTH)


def _bd_to_state(st):
    b = st.shape[0]
    s5 = st.reshape(b, N_HEADS, HEAD_DIM, N_HEADS, HEAD_DIM)
    diag = jnp.stack([s5[:, h, :, h, :] for h in range(N_HEADS)], axis=1)
    return jnp.swapaxes(diag, 2, 3)


def _cumsum_kernel(lf_ref, tri_ref, crep_ref, ct_ref, carry_scr):
    @pl.when(pl.program_id(1) == 0)
    def _():
        carry_scr[...] = jnp.zeros_like(carry_scr)

    c = jnp.dot(tri_ref[...], lf_ref[0], precision=HIGHEST, preferred_element_type=F32) + carry_scr[...]
    carry_scr[...] = c[c.shape[0] - 1:, :]
    c2 = c * LOG2E
    for h in range(N_HEADS):
        crep_ref[0, h] = jnp.broadcast_to(c2[:, h:h + 1], c2.shape)
    ct_ref[0] = c2.T[0:8, :]


def _fox_cumsum(lf, tm):
    b, l, _ = lf.shape
    tri = jnp.asarray(np.tril(np.ones((tm, tm))), F32)
    return pl.pallas_call(
        _cumsum_kernel,
        grid=(b, l // tm),
        in_specs=[pl.BlockSpec((1, tm, LANES), lambda i, j: (i, j, 0)),
                  pl.BlockSpec((tm, tm), lambda i, j: (0, 0))],
        out_specs=[pl.BlockSpec((1, N_HEADS, tm, LANES), lambda i, j: (i, 0, j, 0)),
                   pl.BlockSpec((1, 8, tm), lambda i, j: (i, 0, j))],
        out_shape=[jax.ShapeDtypeStruct((b, N_HEADS, l, LANES), F32), jax.ShapeDtypeStruct((b, 8, l), F32)],
        scratch_shapes=[pltpu.VMEM((1, LANES), F32)],
        compiler_params=_cparams(("arbitrary", "arbitrary")),
        name="fox_cumsum",
    )(lf, tri)


def _fox_prompt_kernel(qt_ref, k_ref, vt_ref, crep_ref, ct_ref, o_ref, qm_scr, m_scr, l_scr, acc_scr, *, tq):
    qi = pl.program_id(1)
    ki = pl.program_id(2)

    @pl.when(ki == 0)
    def _():
        qt = qt_ref[0] * (LOG2E * HEAD_DIM ** -0.5)
        row_head = lax.broadcasted_iota(jnp.int32, qt.shape, 0) // HEAD_DIM
        for h in range(N_HEADS):
            qm_scr[h] = jnp.where(row_head == h, qt, 0.0).astype(BF16)
        m_scr[...] = jnp.full_like(m_scr, -jnp.inf)
        l_scr[...] = jnp.zeros_like(l_scr)
        acc_scr[...] = jnp.zeros_like(acc_scr)

    def block(masked):
        k = k_ref[0].astype(BF16)
        for h in range(N_HEADS):
            rows = slice(h * HEAD_DIM, (h + 1) * HEAD_DIM)
            s = jnp.dot(k, qm_scr[h], preferred_element_type=F32)
            s = s - jnp.tile(crep_ref[0, h], (1, tq // LANES)) + ct_ref[0, h:h + 1, :]
            if masked:
                key = lax.broadcasted_iota(jnp.int32, (tq, tq), 0)
                qry = lax.broadcasted_iota(jnp.int32, (tq, tq), 1)
                s = jnp.where(key <= qry, s, -jnp.inf)
            m_prev = m_scr[h:h + 1, :]
            m_new = jnp.maximum(m_prev, jnp.max(s, axis=0, keepdims=True))
            alpha = jnp.exp2(m_prev - m_new)
            p = jnp.exp2(s - m_new)
            l_scr[h:h + 1, :] = alpha * l_scr[h:h + 1, :] + jnp.sum(p, axis=0, keepdims=True)
            acc_scr[rows, :] = alpha * acc_scr[rows, :] + jnp.dot(vt_ref[0, rows, :].astype(BF16), p.astype(BF16),
                                                                  preferred_element_type=F32)
            m_scr[h:h + 1, :] = m_new

    @pl.when(ki < qi)
    def _():
        block(False)

    @pl.when(ki == qi)
    def _():
        block(True)
        for h in range(N_HEADS):
            rows = slice(h * HEAD_DIM, (h + 1) * HEAD_DIM)
            acc_scr[rows, :] = acc_scr[rows, :] / l_scr[h:h + 1, :]
        o_ref[0] = acc_scr[...].T


def _fox_prompt(fq, fk, fv, crep, ct, tq):
    b, l, _ = fq.shape
    n = l // tq
    qt = jnp.swapaxes(fq, 1, 2)
    vt = jnp.swapaxes(fv, 1, 2)
    kv = lambda i, j, k: jnp.minimum(k, j)
    return pl.pallas_call(
        functools.partial(_fox_prompt_kernel, tq=tq),
        grid=(b, n, n),
        in_specs=[pl.BlockSpec((1, GROUP_WIDTH, tq), lambda i, j, k: (i, 0, j)),
                  pl.BlockSpec((1, tq, GROUP_WIDTH), lambda i, j, k: (i, kv(i, j, k), 0)),
                  pl.BlockSpec((1, GROUP_WIDTH, tq), lambda i, j, k: (i, 0, kv(i, j, k))),
                  pl.BlockSpec((1, N_HEADS, tq, LANES), lambda i, j, k: (i, 0, kv(i, j, k), 0)),
                  pl.BlockSpec((1, 8, tq), lambda i, j, k: (i, 0, j))],
        out_specs=pl.BlockSpec((1, tq, GROUP_WIDTH), lambda i, j, k: (i, j, 0)),
        out_shape=jax.ShapeDtypeStruct((b, l, GROUP_WIDTH), F32),
        scratch_shapes=[pltpu.VMEM((N_HEADS, GROUP_WIDTH, tq), BF16), pltpu.VMEM((8, tq), F32),
                        pltpu.VMEM((8, tq), F32), pltpu.VMEM((GROUP_WIDTH, tq), F32)],
        compiler_params=_cparams(("arbitrary", "arbitrary", "arbitrary")),
        name="fox_prompt",
    )(qt, fk, vt, crep, ct)


PAGES_PER_CHUNK = 16
CHUNK_ROWS = PAGES_PER_CHUNK * PAGE_SIZE


def _fox_paged_kernel(pt_ref, q_ref, kn_ref, vn_ref, lfn_ref, mc_ref, mtot_ref, trix_ref,
                      ck_hbm, cv_hbm, clf_hbm, o_ref,
                      lf_scr, c_scr, kbuf, vbuf, kn_scr, vn_scr, m_scr, l_scr, acc_scr, sem_lf, sem_kv,
                      *, n_pages, page0):
    bi = pl.program_id(0)
    n_chunks = n_pages // PAGES_PER_CHUNK
    n_new = q_ref.shape[1]
    n_rows = N_HEADS * n_new

    def lf_copy(j):
        pg = pt_ref[bi, j] + page0
        return pltpu.make_async_copy(clf_hbm.at[pl.ds(pg, 1), :], lf_scr.at[pl.ds(j, 1), :], sem_lf.at[0])

    def kv_copies(chunk, slot, p):
        pg = pt_ref[bi, chunk * PAGES_PER_CHUNK + p] + page0
        toks = pl.ds(p * PAGE_SIZE, PAGE_SIZE)
        return (pltpu.make_async_copy(ck_hbm.at[pg], kbuf.at[slot, :, :, toks], sem_kv.at[0, slot]),
                pltpu.make_async_copy(cv_hbm.at[pg], vbuf.at[slot, :, :, toks], sem_kv.at[1, slot]))

    def start_chunk(chunk, slot):
        for p in range(PAGES_PER_CHUNK):
            for cp in kv_copies(chunk, slot, p):
                cp.start()

    def wait_chunk(chunk, slot):
        for p in range(PAGES_PER_CHUNK):
            for cp in kv_copies(chunk, slot, p):
                cp.wait()

    def lf_start(j, carry):
        lf_copy(j).start()
        return carry

    def lf_wait(j, carry):
        lf_copy(j).wait()
        return carry

    lax.fori_loop(0, n_pages, lf_start, 0)
    start_chunk(0, 0)

    lf_scr[n_pages:, :] = jnp.zeros((lf_scr.shape[0] - n_pages, lf_scr.shape[1]), F32)
    lf_scr[n_pages:n_pages + 1, :] = lfn_ref[0]
    kn_scr[...] = jnp.zeros_like(kn_scr)
    vn_scr[...] = jnp.zeros_like(vn_scr)
    kn_scr[0:n_new, :] = kn_ref[0]
    vn_scr[0:n_new, :] = vn_ref[0]
    m_scr[...] = jnp.full_like(m_scr, -jnp.inf)
    l_scr[...] = jnp.zeros_like(l_scr)
    acc_scr[...] = jnp.zeros_like(acc_scr)

    lax.fori_loop(0, n_pages, lf_wait, 0)
    lf = lf_scr[...]
    c_local = jnp.dot(lf, mc_ref[...], precision=HIGHEST, preferred_element_type=F32)
    tot = jnp.dot(lf, mtot_ref[...], precision=HIGHEST, preferred_element_type=F32)
    c2d = c_local + jnp.dot(trix_ref[...], tot, precision=HIGHEST, preferred_element_type=F32)
    for j in range(n_pages + 1):
        for h in range(N_HEADS):
            c_scr[h:h + 1, j * PAGE_SIZE:(j + 1) * PAGE_SIZE] = c2d[j:j + 1, h * PAGE_SIZE:(h + 1) * PAGE_SIZE]

    q4 = q_ref[0] * (HEAD_DIM ** -0.5)
    knt = kn_scr[...].T
    vnt = vn_scr[...].T
    lane = lax.broadcasted_iota(jnp.int32, (n_new, PAGE_SIZE), 1)
    tok = lax.broadcasted_iota(jnp.int32, (n_new, PAGE_SIZE), 0)
    new0 = n_pages * PAGE_SIZE

    def c_keys(h, start, width):
        return jnp.broadcast_to(c_scr[h:h + 1, pl.ds(start, width)], (n_new, width))

    def split(a):
        hi = a.astype(BF16)
        return hi, (a - hi.astype(F32)).astype(BF16)

    def mm3(a, b, dims):
        a_hi, a_lo = split(a)
        b_hi, b_lo = split(b)
        n = a.shape[0]
        both = lax.dot_general(jnp.concatenate([a_hi, a_lo], axis=0), b_hi, dims, preferred_element_type=F32)
        return both[0:n] + both[n:2 * n] + lax.dot_general(a_hi, b_lo, dims, preferred_element_type=F32)

    def attend(kt, vt, start, width, masked):
        for h in range(N_HEADS):
            r = slice(h * n_new, (h + 1) * n_new)
            d = slice(h * HEAD_DIM, (h + 1) * HEAD_DIM)
            cq = jnp.sum(jnp.where(lane == tok, c_keys(h, new0, PAGE_SIZE), 0.0), axis=-1, keepdims=True)
            s = mm3(q4[:, d], kt(h), _NN) + cq - c_keys(h, start, width)
            if masked:
                s = jnp.where(lane <= tok, s, -jnp.inf)
            m_prev = m_scr[r]
            m_new = jnp.maximum(m_prev, jnp.max(s, axis=-1, keepdims=True))
            alpha = jnp.exp(m_prev - m_new)
            p = jnp.exp(s - m_new)
            l_scr[r] = alpha * l_scr[r] + jnp.sum(p, axis=-1, keepdims=True)
            acc_scr[r] = alpha * acc_scr[r] + mm3(p, vt(h), _NT)
            m_scr[r] = m_new

    def chunk_body(chunk, carry):
        slot = chunk % 2
        wait_chunk(chunk, slot)

        @pl.when(chunk + 1 < n_chunks)
        def _():
            start_chunk(chunk + 1, 1 - slot)

        start = pl.multiple_of(chunk * CHUNK_ROWS, CHUNK_ROWS)
        attend(lambda h: kbuf[slot, h], lambda h: vbuf[slot, h], start, CHUNK_ROWS, False)
        return carry

    lax.fori_loop(0, n_chunks, chunk_body, 0)
    head_rows = lambda a: (lambda h: a[h * HEAD_DIM:(h + 1) * HEAD_DIM, :])
    attend(head_rows(knt), head_rows(vnt), new0, PAGE_SIZE, True)

    for h in range(N_HEADS):
        r = slice(h * n_new, (h + 1) * n_new)
        o_ref[0, :, h * HEAD_DIM:(h + 1) * HEAD_DIM] = acc_scr[r] / l_scr[r]


def _fox_paged(page_table, fq, fk, fv, lf_new_row, cache_k, cache_v, cache_lf, layer):
    s, n_new, _ = fq.shape
    n_pages = page_table.shape[1]
    n_pool = cache_k.shape[0] // 2
    n_rows = N_HEADS * n_new
    c_rows = ((n_pages + 1 + 7) // 8) * 8
    lanes = PAGE_SIZE * N_HEADS
    src = np.arange(lanes)
    dst = np.arange(lanes)
    same_head = (src[:, None] % N_HEADS) == (dst[None, :] // PAGE_SIZE)
    mc = jnp.asarray(same_head & ((src[:, None] // N_HEADS) <= (dst[None, :] % PAGE_SIZE)), F32)
    mtot = jnp.asarray(same_head, F32)
    trix = jnp.asarray(np.tril(np.ones((c_rows, c_rows)), -1), F32)
    tokspec = lambda w: pl.BlockSpec((1, n_new, w), lambda i, pt: (i, 0, 0))
    full = lambda a: pl.BlockSpec(a.shape, lambda i, pt: (0,) * a.ndim)
    anyspec = pl.BlockSpec(memory_space=pl.ANY)
    grid_spec = pltpu.PrefetchScalarGridSpec(
        num_scalar_prefetch=1,
        grid=(s,),
        in_specs=[tokspec(GROUP_WIDTH), tokspec(GROUP_WIDTH), tokspec(GROUP_WIDTH),
                  pl.BlockSpec((1, 1, lanes), lambda i, pt: (i, 0, 0)),
                  full(mc), full(mtot), full(trix), anyspec, anyspec, anyspec],
        out_specs=tokspec(GROUP_WIDTH),
        scratch_shapes=[pltpu.VMEM((c_rows, lanes), F32), pltpu.VMEM((8, (n_pages + 1) * PAGE_SIZE), F32),
                        pltpu.VMEM((2, N_HEADS, HEAD_DIM, CHUNK_ROWS), F32),
                        pltpu.VMEM((2, N_HEADS, HEAD_DIM, CHUNK_ROWS), F32),
                        pltpu.VMEM((PAGE_SIZE, GROUP_WIDTH), F32), pltpu.VMEM((PAGE_SIZE, GROUP_WIDTH), F32),
                        pltpu.VMEM((n_rows, 1), F32), pltpu.VMEM((n_rows, 1), F32),
                        pltpu.VMEM((n_rows, HEAD_DIM), F32),
                        pltpu.SemaphoreType.DMA((1,)), pltpu.SemaphoreType.DMA((2, 2))],
    )
    return pl.pallas_call(
        functools.partial(_fox_paged_kernel, n_pages=n_pages, page0=layer * n_pool),
        grid_spec=grid_spec,
        out_shape=jax.ShapeDtypeStruct((s, n_new, GROUP_WIDTH), F32),
        compiler_params=_cparams(("arbitrary",)),
        name="fox_paged",
    )(page_table, fq, fk, fv, lf_new_row, mc, mtot, trix, cache_k, cache_v, cache_lf)


def _convpool_kernel(u_ref, pu_ref, cpre_ref, ppre_ref, cw_ref, cb_ref, lng_ref, lnb_ref, pw_ref, ps_ref,
                     oc_ref, op_ref, cext, pext, *, tm, pos0, precise):
    i = pl.program_id(1)

    @pl.when(i == 0)
    def _():
        cext[0:CONV_HALO, :] = cpre_ref[0]
        pext[0:POOL_MAX, :] = ppre_ref[0]

    @pl.when(i > 0)
    def _():
        cext[0:CONV_HALO, :] = cext[tm:tm + CONV_HALO, :]
        pext[0:POOL_MAX, :] = pext[tm:tm + POOL_MAX, :]

    cext[CONV_HALO:CONV_HALO + tm, :] = u_ref[0]
    pext[POOL_MAX:POOL_MAX + tm, :] = pu_ref[0]

    off = CONV_HALO - (CONV_WIDTH - 1)
    y = cext[off:off + tm, :] * cw_ref[0:1, :]
    for j in range(1, CONV_WIDTH):
        y = y + cext[off + j:off + j + tm, :] * cw_ref[j:j + 1, :]
    y = y + cb_ref[...]
    mu = jnp.mean(y, axis=-1, keepdims=True)
    yc = y - mu
    var = jnp.mean(yc * yc, axis=-1, keepdims=True)
    yn = yc * lax.rsqrt(var + EPS) * lng_ref[...] + lnb_ref[...]
    oc_ref[0] = yn * jax.nn.sigmoid(yn)

    lane_group = lax.broadcasted_iota(jnp.int32, (tm, GROUP_WIDTH), 1) // (GROUP_WIDTH // len(POOL_WINDOWS))
    pos = pos0 + i * tm + lax.broadcasted_iota(jnp.int32, (tm, 1), 0)
    x0 = pext[POOL_MAX:POOL_MAX + tm, :]
    run = x0
    mean = jnp.zeros((tm, GROUP_WIDTH), F32)
    for back in range(1, POOL_MAX):
        run = run + pext[POOL_MAX - back:POOL_MAX - back + tm, :]
        if back + 1 in POOL_WINDOWS:
            g = POOL_WINDOWS.index(back + 1)
            cnt = jnp.minimum(back + 1, pos + 1).astype(F32)
            mean = jnp.where(lane_group == g, run / cnt, mean)
    d = mean - x0
    op_ref[0] = _mm(d, pw_ref[...], precise) * ps_ref[...]


def _convpool(u, pu, conv_prefix, pool_prefix, lp, tm, pos0, precise):
    b, l, _ = u.shape
    seq = pl.BlockSpec((1, tm, GROUP_WIDTH), lambda i, j: (i, j, 0))
    full = lambda a: pl.BlockSpec(a.shape, lambda i, j: (0,) * a.ndim)
    consts = (lp['conv_w'], lp['conv_b'], lp['conv_ln_g'], lp['conv_ln_b'],
              lp['pool_w_f32'] if precise else lp['pool_w'], lp['pool_scale'])
    return pl.pallas_call(
        functools.partial(_convpool_kernel, tm=tm, pos0=pos0, precise=precise),
        grid=(b, l // tm),
        in_specs=[seq, seq,
                  pl.BlockSpec((1, CONV_HALO, GROUP_WIDTH), lambda i, j: (i, 0, 0)),
                  pl.BlockSpec((1, POOL_MAX, GROUP_WIDTH), lambda i, j: (i, 0, 0))] + [full(a) for a in consts],
        out_specs=[seq, seq],
        out_shape=[jax.ShapeDtypeStruct((b, l, GROUP_WIDTH), F32)] * 2,
        scratch_shapes=[pltpu.VMEM((CONV_HALO + tm, GROUP_WIDTH), F32), pltpu.VMEM((POOL_MAX + tm, GROUP_WIDTH), F32)],
        compiler_params=_cparams(("arbitrary", "arbitrary")),
        name="convpool",
    )(u, pu, conv_prefix, pool_prefix, *consts)


def _router_logits(h2, wr):
    return jnp.dot(h2, wr, precision=HIGHEST, preferred_element_type=F32)


def _outproj_kernel(x_ref, og_ref, of_ref, oc_ref, op_ref, w_ref, g2_ref, wr_ref, br_ref,
                    x1_ref, h2_ref, lg_ref, *, precise):
    acc = x_ref[...]
    for j, m_ref in enumerate((og_ref, of_ref, oc_ref, op_ref)):
        acc = acc + _mm(m_ref[...], w_ref[j * GROUP_WIDTH:(j + 1) * GROUP_WIDTH, :], precise)
    x1_ref[...] = acc
    ms = jnp.mean(acc * acc, axis=-1, keepdims=True)
    h2 = acc * lax.rsqrt(ms + EPS) * g2_ref[...]
    h2_ref[...] = h2.astype(h2_ref.dtype)
    lg_ref[...] = _router_logits(h2, wr_ref[...]) + br_ref[...]


def _outproj(x2d, og, of, oc, op, lp, tm, precise):
    t = x2d.shape[0]
    row = lambda w: pl.BlockSpec((tm, w), lambda i: (i, 0))
    full = lambda a: pl.BlockSpec(a.shape, lambda i: (0,) * a.ndim)
    consts = (lp['w_out_f32'] if precise else lp['w_out'], lp['g2'], lp['w_router'], lp['b_router'])
    return pl.pallas_call(
        functools.partial(_outproj_kernel, precise=precise),
        grid=(t // tm,),
        in_specs=[row(D_MODEL)] + [row(GROUP_WIDTH)] * 4 + [full(a) for a in consts],
        out_specs=[row(D_MODEL), row(D_MODEL), row(LANES)],
        out_shape=[jax.ShapeDtypeStruct((t, D_MODEL), F32),
                   jax.ShapeDtypeStruct((t, D_MODEL), F32 if precise else BF16),
                   jax.ShapeDtypeStruct((t, LANES), F32)],
        compiler_params=_cparams(("arbitrary",)),
        name="outproj",
    )(x2d, og, of, oc, op, *consts)


def _route(lg):
    lane = lax.broadcasted_iota(jnp.int32, lg.shape, 1)
    big = jnp.int32(LANES)
    neg = -jnp.inf
    is_g = lane < MOE_GROUPS
    gl = jnp.where(is_g, lg, neg)
    gmax = jnp.max(gl, axis=-1, keepdims=True)
    gidx = jnp.min(jnp.where(gl == gmax, lane, big), axis=-1, keepdims=True)
    p_group = 1.0 / jnp.sum(jnp.where(is_g, jnp.exp(lg - gmax), 0.0), axis=-1, keepdims=True)
    lo = ROUTER_LANE0 + gidx * EXPERTS_PER_GROUP
    el = jnp.where((lane >= lo) & (lane < lo + EXPERTS_PER_GROUP), lg, neg)
    v1 = jnp.max(el, axis=-1, keepdims=True)
    i1 = jnp.min(jnp.where(el == v1, lane, big), axis=-1, keepdims=True)
    el2 = jnp.where(lane == i1, neg, el)
    v2 = jnp.max(el2, axis=-1, keepdims=True)
    i2 = jnp.min(jnp.where(el2 == v2, lane, big), axis=-1, keepdims=True)
    e21 = jnp.exp(v2 - v1)
    g1 = 1.0 / (1.0 + e21)
    g2 = e21 * g1
    return jnp.where(lane == i1, g1 * p_group, 0.0) + jnp.where(lane == i2, g2 * p_group, 0.0)


def _moe_dense_kernel(h_ref, lg_ref, x1_ref, wg_ref, wu_ref, wd_ref, o_ref, comb_scr, acc_scr, *, precise):
    e = pl.program_id(1)

    @pl.when(e == 0)
    def _():
        comb_scr[...] = _route(lg_ref[...])
        acc_scr[...] = x1_ref[...]

    h = h_ref[...]
    a = _mm(h, wg_ref[0], precise)
    b = _mm(h, wu_ref[0], precise)
    y = _mm(a * jax.nn.sigmoid(a) * b, wd_ref[0], precise)
    lane = lax.broadcasted_iota(jnp.int32, comb_scr.shape, 1)
    cw = jnp.sum(jnp.where(lane == e + ROUTER_LANE0, comb_scr[...], 0.0), axis=-1, keepdims=True)
    acc_scr[...] = acc_scr[...] + cw * y

    @pl.when(e == N_EXPERTS - 1)
    def _():
        o_ref[...] = acc_scr[...]


def _moe_dense(h2, lg, x1, lp, tm, precise):
    t = h2.shape[0]
    row = lambda w: pl.BlockSpec((tm, w), lambda i, e: (i, 0))
    sfx = '_f32' if precise else ''
    return pl.pallas_call(
        functools.partial(_moe_dense_kernel, precise=precise),
        grid=(t // tm, N_EXPERTS),
        in_specs=[row(D_MODEL), row(LANES), row(D_MODEL),
                  pl.BlockSpec((1, D_MODEL, D_EXPERT), lambda i, e: (e, 0, 0)),
                  pl.BlockSpec((1, D_MODEL, D_EXPERT), lambda i, e: (e, 0, 0)),
                  pl.BlockSpec((1, D_EXPERT, D_MODEL), lambda i, e: (e, 0, 0))],
        out_specs=row(D_MODEL),
        out_shape=jax.ShapeDtypeStruct((t, D_MODEL), F32),
        scratch_shapes=[pltpu.VMEM((tm, LANES), F32), pltpu.VMEM((tm, D_MODEL), F32)],
        compiler_params=_cparams(("arbitrary", "arbitrary")),
        name="moe_dense",
    )(h2, lg, x1, lp['exp_w_gate' + sfx], lp['exp_w_up' + sfx], lp['exp_w_down' + sfx])


def _block_diag(blocks):
    n, r, c = blocks.shape
    out = jnp.zeros((n, r, n, c), blocks.dtype)
    for g in range(n):
        out = out.at[g, :, g, :].set(blocks[g])
    return out.reshape(n * r, n * c)


def _prep_layer(l, p):
    w = p['w_in'][l]
    o = np.cumsum((0, 256, 256, 256, 256, GLA_RANK, 256, 256, 256, N_HEADS, 256, 256, 256))
    small = jnp.concatenate([w[:, o[8]:o[9]], w[:, o[4]:o[5]],
                             jnp.zeros((D_MODEL, LANES - N_HEADS - GLA_RANK), F32)], axis=1)
    w_packed = jnp.concatenate([w[:, o[0]:o[4]], w[:, o[5]:o[8]], w[:, o[9]:o[12]], small], axis=1)
    wa2 = jnp.zeros((LANES, GROUP_WIDTH), F32).at[SMALL_GA0:SMALL_GA0 + GLA_RANK].set(p['gla_w_a2'][l])
    bf = jnp.zeros((1, LANES), F32).at[0, SMALL_FF0:SMALL_FF0 + N_HEADS].set(p['fox_b_f'][l])
    ones_bd = _block_diag(jnp.ones((N_HEADS, HEAD_DIM, HEAD_DIM), F32))
    w_router = jnp.concatenate([p['router_wg'][l], p['router_we'][l],
                                jnp.zeros((D_MODEL, LANES - MOE_GROUPS - N_EXPERTS), F32)], axis=1)
    b_router = jnp.concatenate([p['router_bg'][l], p['router_be'][l],
                                jnp.zeros((LANES - MOE_GROUPS - N_EXPERTS,), F32)])[None, :]
    conv_w = jnp.concatenate([p['conv_w'][l], jnp.zeros((CONV_HALO - CONV_WIDTH, GROUP_WIDTH), F32)], axis=0)
    row = lambda a: a.reshape(1, -1).astype(F32)
    return {
        'g1': row(p['norm1_g'][l]), 'w_in': w_packed.astype(BF16), 'wa2': wa2, 'ba': row(p['gla_b_a'][l]), 'bf': bf,
        'gq': row(jnp.tile(p['fox_q_norm_g'][l], N_HEADS)), 'gk': row(jnp.tile(p['fox_k_norm_g'][l], N_HEADS)),
        'bd_mean': ones_bd * (1.0 / HEAD_DIM), 'bd_ones': ones_bd, 'gla_gn': row(p['gla_norm_g'][l]),
        'conv_w': conv_w, 'conv_b': row(p['conv_b'][l]), 'conv_ln_g': row(p['conv_ln_g'][l]),
        'conv_ln_b': row(p['conv_ln_b'][l]), 'pool_w': _block_diag(p['pool_w'][l]).astype(BF16),
        'pool_scale': row(p['pool_scale'][l]), 'w_out': p['w_out'][l].astype(BF16), 'g2': row(p['norm2_g'][l]),
        'w_router': w_router, 'b_router': b_router,
        'exp_w_gate': p['exp_w_gate'][l].astype(BF16), 'exp_w_up': p['exp_w_up'][l].astype(BF16),
        'exp_w_down': p['exp_w_down'][l].astype(BF16),
        'w_in_f32': w_packed, 'pool_w_f32': _block_diag(p['pool_w'][l].astype(F32)), 'w_out_f32': p['w_out'][l],
        'exp_w_gate_f32': p['exp_w_gate'][l], 'exp_w_up_f32': p['exp_w_up'][l], 'exp_w_down_f32': p['exp_w_down'][l],
    }


def _pad_rows_front(a, rows):
    return jnp.pad(a.astype(F32), ((0, 0), (rows - a.shape[1], 0), (0, 0)))


def _flat(a):
    return a.reshape(-1, a.shape[-1])


def _layer_prompt(lp, xp, bsz, seq):
    tp = 512
    gla_p, la_p, fq_p, fk_p, fv_p, lf_p, u_p, pu_p = _inproj(xp, lp, tp, False)
    seq3 = lambda a: a.reshape(bsz, seq, a.shape[-1])
    og_p, st_p = _gla(seq3(gla_p), seq3(la_p), jnp.zeros((bsz, GROUP_WIDTH, GROUP_WIDTH), F32), lp, 128, False)
    c_p, ct_p = _fox_cumsum(seq3(lf_p), 512)
    of_p = _fox_prompt(seq3(fq_p), seq3(fk_p), seq3(fv_p), c_p, ct_p, 512)
    oc_p, op_p = _convpool(seq3(u_p), seq3(pu_p), jnp.zeros((bsz, CONV_HALO, GROUP_WIDTH), F32),
                           jnp.zeros((bsz, POOL_MAX, GROUP_WIDTH), F32), lp, 512, 0, False)
    x1_p, h2_p, lg_p = _outproj(xp, _flat(og_p), _flat(of_p), _flat(oc_p), _flat(op_p), lp, tp, False)
    xp_new = _moe_dense(h2_p, lg_p, x1_p, lp, 1024, False)
    heads = lambda a: a.reshape(bsz, seq, N_HEADS, HEAD_DIM)
    outs = (heads(fk_p), heads(fv_p), seq3(lf_p)[:, :, :N_HEADS], _bd_to_state(st_p),
            seq3(u_p)[:, -(CONV_WIDTH - 1):], seq3(pu_p)[:, -(POOL_MAX - 1):])
    return xp_new, outs


def _layer_sample(l, lp, xs, dec_b, dec_seq, page_table, caches, state_gla, state_conv, state_pool):
    cache_k, cache_v, cache_lf = caches
    past = page_table.shape[1] * PAGE_SIZE
    flat = _flat
    ts = xs.shape[0]
    gla_s, la_s, fq_s, fk_s, fv_s, lf_s, u_s, pu_s = _inproj(xs, lp, ts, True)
    dec3 = lambda a: a.reshape(dec_b, dec_seq, a.shape[-1])
    pad_t = lambda a: jnp.pad(dec3(a), ((0, 0), (0, GLA_SUB - dec_seq), (0, 0)))
    og_s, st_s = _gla(pad_t(gla_s), pad_t(la_s), _state_to_bd(state_gla[l]), lp, GLA_SUB, True)
    og_s = og_s[:, :dec_seq]
    lf_row = jnp.pad(lf_s[:, :N_HEADS].reshape(dec_b, 1, dec_seq * N_HEADS),
                     ((0, 0), (0, 0), (0, PAGE_SIZE * N_HEADS - dec_seq * N_HEADS)))
    of_s = _fox_paged(page_table, dec3(fq_s), dec3(fk_s), dec3(fv_s), lf_row, cache_k, cache_v, cache_lf, l)
    oc_s, op_s = _convpool(dec3(u_s), dec3(pu_s), _pad_rows_front(state_conv[l], CONV_HALO),
                           _pad_rows_front(state_pool[l], POOL_MAX), lp, dec_seq, past, True)
    x1_s, h2_s, lg_s = _outproj(xs, flat(og_s), flat(of_s), flat(oc_s), flat(op_s), lp, ts, True)
    xs_new = _moe_dense(h2_s, lg_s, x1_s, lp, ts, True)
    heads = lambda a: a.reshape(dec_b, dec_seq, N_HEADS, HEAD_DIM)
    conv_s = jnp.concatenate([state_conv[l].astype(F32), dec3(u_s)], axis=1)[:, -(CONV_WIDTH - 1):]
    pool_s = jnp.concatenate([state_pool[l].astype(F32), dec3(pu_s)], axis=1)[:, -(POOL_MAX - 1):]
    outs = (heads(fk_s), heads(fv_s), dec3(lf_s)[:, :, :N_HEADS], _bd_to_state(st_s), conv_s, pool_s)
    return xs_new, outs


def kernel(x_prompt, x_sample, cache_k, cache_v, cache_logf, page_table, state_gla, state_conv, state_pool,
           norm1_g, w_in, gla_w_a2, gla_b_a, gla_norm_g, fox_b_f, fox_q_norm_g, fox_k_norm_g,
           conv_w, conv_b, conv_ln_g, conv_ln_b, pool_w, pool_scale, w_out, norm2_g,
           router_wg, router_bg, router_we, router_be, exp_w_gate, exp_w_up, exp_w_down):
    params = dict(norm1_g=norm1_g, w_in=w_in, gla_w_a2=gla_w_a2, gla_b_a=gla_b_a, gla_norm_g=gla_norm_g,
                  fox_b_f=fox_b_f, fox_q_norm_g=fox_q_norm_g, fox_k_norm_g=fox_k_norm_g, conv_w=conv_w,
                  conv_b=conv_b, conv_ln_g=conv_ln_g, conv_ln_b=conv_ln_b, pool_w=pool_w, pool_scale=pool_scale,
                  w_out=w_out, norm2_g=norm2_g, router_wg=router_wg, router_bg=router_bg, router_we=router_we,
                  router_be=router_be, exp_w_gate=exp_w_gate, exp_w_up=exp_w_up, exp_w_down=exp_w_down)
    depth = w_in.shape[0]
    bsz, seq, _ = x_prompt.shape
    dec_b, dec_seq, _ = x_sample.shape
    n_pool = cache_k.shape[1]
    paged = lambda c: jnp.transpose(c, (0, 1, 3, 4, 2)).reshape(depth * n_pool, N_HEADS, HEAD_DIM, PAGE_SIZE)
    caches = (paged(cache_k), paged(cache_v), cache_logf.astype(F32).reshape(depth * n_pool, PAGE_SIZE * N_HEADS))
    xp = x_prompt.reshape(bsz * seq, D_MODEL)
    xs = x_sample.reshape(dec_b * dec_seq, D_MODEL)
    per_layer = []
    for l in range(depth):
        lp = _prep_layer(l, params)
        xp, (kp, vp, lfp, glap, convp, poolp) = _layer_prompt(lp, xp, bsz, seq)
        xs, (ks, vs, lfs, glas, convs, pools) = _layer_sample(l, lp, xs, dec_b, dec_seq, page_table, caches,
                                                              state_gla, state_conv, state_pool)
        per_layer.append((kp, vp, lfp, ks, vs, lfs, glap, glas, convp, convs, poolp, pools))
    stacked = tuple(jnp.stack([per_layer[l][i] for l in range(depth)]) for i in range(12))
    return (xp.reshape(bsz, seq, D_MODEL), xs.reshape(dec_b, dec_seq, D_MODEL)) + stacked
```

```python
import functools

import jax
import jax.numpy as jnp
import numpy as np
from jax import lax
from jax.experimental import pallas as pl
from jax.experimental.pallas import tpu as pltpu

F32 = jnp.float32
BF16 = jnp.bfloat16
HIGHEST = lax.Precision.HIGHEST

D_MODEL = 1024
GROUP_WIDTH = 256
HEAD_DIM = 64
N_HEADS = GROUP_WIDTH // HEAD_DIM
GLA_RANK = 16
GLA_TAU = 16.0
GLA_SUB = 16
PAGE_SIZE = 128
CONV_WIDTH = 31
CONV_HALO = 32
POOL_WINDOWS = (2, 4, 8, 16)
POOL_MAX = 16
MOE_GROUPS = 4
EXPERTS_PER_GROUP = 4
N_EXPERTS = MOE_GROUPS * EXPERTS_PER_GROUP
D_EXPERT = 512
EPS = 1e-6
LOG2E = 1.4426950408889634
LANES = 128
ROUTER_LANE0 = MOE_GROUPS
VMEM_LIMIT = 56 * 1024 * 1024

C_GLA, C_FQ, C_FK, C_FV, C_CA, C_CB, C_PU, C_SMALL, C_END = 0, 1024, 1280, 1536, 1792, 2048, 2304, 2560, 2688
SMALL_FF0 = 0
SMALL_GA0 = N_HEADS


def _cparams(sem):
    return pltpu.CompilerParams(dimension_semantics=sem, vmem_limit_bytes=VMEM_LIMIT)


def _log_sigmoid(x):
    return jnp.minimum(x, 0.0) - jnp.log1p(jnp.exp(-jnp.abs(x)))


_NN = (((1,), (0,)), ((), ()))
_NT = (((1,), (1,)), ((), ()))
_TN = (((0,), (0,)), ((), ()))


def _mm(a, b, precise, dims=_NN):
    if precise:
        return lax.dot_general(a.astype(F32), b.astype(F32), dims, precision=HIGHEST, preferred_element_type=F32)
    return lax.dot_general(a.astype(BF16), b.astype(BF16), dims, preferred_element_type=F32)


def _head_rms(x, bd_mean, g):
    ms = jnp.dot(x * x, bd_mean, precision=HIGHEST, preferred_element_type=F32)
    return x * lax.rsqrt(ms + EPS) * g


def _inproj_kernel(x_ref, g1_ref, w_ref, wa2_ref, ba_ref, bf_ref, gq_ref, gk_ref, bd_ref,
                   gla_ref, la_ref, fq_ref, fk_ref, fv_ref, lf_ref, u_ref, pu_ref, h_scr, *, precise):
    x = x_ref[...]
    ms = jnp.mean(x * x, axis=-1, keepdims=True)
    h_scr[...] = (x * lax.rsqrt(ms + EPS) * g1_ref[...]).astype(h_scr.dtype)

    def proj(c0, width):
        return _mm(h_scr[...], w_ref[:, c0:c0 + width], precise)

    for j in range(4):
        gla_ref[:, j * GROUP_WIDTH:(j + 1) * GROUP_WIDTH] = proj(C_GLA + j * GROUP_WIDTH, GROUP_WIDTH)
    small = proj(C_SMALL, LANES)
    la_pre = jnp.dot(small, wa2_ref[...], precision=HIGHEST, preferred_element_type=F32) + ba_ref[...]
    la_ref[...] = _log_sigmoid(la_pre) * (1.0 / GLA_TAU)
    lf_ref[...] = _log_sigmoid(small + bf_ref[...])
    bd = bd_ref[...]
    fq_ref[...] = _head_rms(proj(C_FQ, GROUP_WIDTH), bd, gq_ref[...])
    fk_ref[...] = _head_rms(proj(C_FK, GROUP_WIDTH), bd, gk_ref[...])
    fv_ref[...] = proj(C_FV, GROUP_WIDTH)
    u_ref[...] = proj(C_CA, GROUP_WIDTH) * jax.nn.sigmoid(proj(C_CB, GROUP_WIDTH))
    pu_ref[...] = proj(C_PU, GROUP_WIDTH)


def _inproj(x2d, lp, tm, precise):
    t = x2d.shape[0]
    row = lambda w: pl.BlockSpec((tm, w), lambda i: (i, 0))
    full = lambda a: pl.BlockSpec(a.shape, lambda i: (0,) * a.ndim)
    w_in = lp['w_in_f32'] if precise else lp['w_in']
    consts = (lp['g1'], w_in, lp['wa2'], lp['ba'], lp['bf'], lp['gq'], lp['gk'], lp['bd_mean'])
    widths = (D_MODEL, GROUP_WIDTH, GROUP_WIDTH, GROUP_WIDTH, GROUP_WIDTH, LANES, GROUP_WIDTH, GROUP_WIDTH)
    return pl.pallas_call(
        functools.partial(_inproj_kernel, precise=precise),
        grid=(t // tm,),
        in_specs=[row(D_MODEL)] + [full(a) for a in consts],
        out_specs=[row(w) for w in widths],
        out_shape=[jax.ShapeDtypeStruct((t, w), F32) for w in widths],
        scratch_shapes=[pltpu.VMEM((tm, D_MODEL), w_in.dtype)],
        compiler_params=_cparams(("arbitrary",)),
        name="inproj",
    )(x2d, *consts)


def _gla_kernel(qkvg_ref, la_ref, st0_ref, gn_ref, bd_ref, bdm_ref, tri_ref, o_ref, stf_ref,
                st_scr, b_scr, o_scr, *, ch, precise):
    c = pl.program_id(1)

    @pl.when(c == 0)
    def _():
        st_scr[...] = st0_ref[0]

    b_scr[...] = jnp.dot(tri_ref[...], la_ref[0], precision=HIGHEST, preferred_element_type=F32)
    rows = lax.broadcasted_iota(jnp.int32, (GLA_SUB, GROUP_WIDTH), 0)
    bd = bd_ref[...]
    mm_dtype = F32 if precise else BF16

    def step(i, carry):
        r0 = pl.multiple_of(i * GLA_SUB, GLA_SUB)
        q = qkvg_ref[0, pl.ds(r0, GLA_SUB), 0:GROUP_WIDTH] * (HEAD_DIM ** -0.5)
        k = qkvg_ref[0, pl.ds(r0, GLA_SUB), GROUP_WIDTH:2 * GROUP_WIDTH]
        v = qkvg_ref[0, pl.ds(r0, GLA_SUB), 2 * GROUP_WIDTH:3 * GROUP_WIDTH]
        b = b_scr[pl.ds(r0, GLA_SUB), :]
        b_last = b[GLA_SUB - 1:GLA_SUB, :]
        st = st_scr[...]
        o_inter = _mm(q * jnp.exp(b), st, precise, _NT)
        dst = _mm(v, k * jnp.exp(b_last - b), precise, _TN)
        st_scr[...] = st * jnp.exp(b_last) + dst * bd
        ws = []
        for s in range(GLA_SUB):
            decay = jnp.where(rows >= s, jnp.exp(b - b[s:s + 1, :]), 0.0)
            ws.append((decay * (q * k[s:s + 1, :])).astype(mm_dtype))
        a_b = _mm(jnp.concatenate(ws, axis=0), bd, precise)
        o_diag = a_b[0:GLA_SUB] * v[0:1, :]
        for s in range(1, GLA_SUB):
            o_diag = o_diag + a_b[s * GLA_SUB:(s + 1) * GLA_SUB] * v[s:s + 1, :]
        o_scr[pl.ds(r0, GLA_SUB), :] = o_inter + o_diag
        return carry

    lax.fori_loop(0, ch // GLA_SUB, step, 0)
    gate = qkvg_ref[0, :, 3 * GROUP_WIDTH:4 * GROUP_WIDTH]
    o_ref[0] = _head_rms(o_scr[...], bdm_ref[...], gn_ref[...]) * (gate * jax.nn.sigmoid(gate))

    @pl.when(c == pl.num_programs(1) - 1)
    def _():
        stf_ref[0] = st_scr[...]


def _gla(qkvg, la, st0, lp, ch, precise):
    b, l, _ = qkvg.shape
    tri = jnp.asarray(np.kron(np.eye(ch // GLA_SUB), np.tril(np.ones((GLA_SUB, GLA_SUB)))), F32)
    full = lambda a: pl.BlockSpec(a.shape, lambda i, j: (0,) * a.ndim)
    consts = (lp['gla_gn'], lp['bd_ones'], lp['bd_mean'], tri)
    return pl.pallas_call(
        functools.partial(_gla_kernel, ch=ch, precise=precise),
        grid=(b, l // ch),
        in_specs=[pl.BlockSpec((1, ch, 4 * GROUP_WIDTH), lambda i, j: (i, j, 0)),
                  pl.BlockSpec((1, ch, GROUP_WIDTH), lambda i, j: (i, j, 0)),
                  pl.BlockSpec((1, GROUP_WIDTH, GROUP_WIDTH), lambda i, j: (i, 0, 0))] + [full(a) for a in consts],
        out_specs=[pl.BlockSpec((1, ch, GROUP_WIDTH), lambda i, j: (i, j, 0)),
                   pl.BlockSpec((1, GROUP_WIDTH, GROUP_WIDTH), lambda i, j: (i, 0, 0))],
        out_shape=[jax.ShapeDtypeStruct((b, l, GROUP_WIDTH), F32),
                   jax.ShapeDtypeStruct((b, GROUP_WIDTH, GROUP_WIDTH), F32)],
        scratch_shapes=[pltpu.VMEM((GROUP_WIDTH, GROUP_WIDTH), F32),
                        pltpu.VMEM((ch, GROUP_WIDTH), F32),
                        pltpu.VMEM((ch, GROUP_WIDTH), F32)],
        compiler_params=_cparams(("arbitrary", "arbitrary")),
        name="gla",
    )(qkvg, la, st0, *consts)


def _state_to_bd(s):
    b = s.shape[0]
    st = jnp.swapaxes(s.astype(F32), 2, 3)
    out = jnp.zeros((b, N_HEADS, HEAD_DIM, N_HEADS, HEAD_DIM), F32)
    for h in range(N_HEADS):
        out = out.at[:, h, :, h, :].set(st[:, h])
    return out.reshape(b, GROUP_WIDTH, GROUP_WIDTH)


def _bd_to_state(st):
    b = st.shape[0]
    s5 = st.reshape(b, N_HEADS, HEAD_DIM, N_HEADS, HEAD_DIM)
    diag = jnp.stack([s5[:, h, :, h, :] for h in range(N_HEADS)], axis=1)
    return jnp.swapaxes(diag, 2, 3)


def _cumsum_kernel(lf_ref, tri_ref, crep_ref, ct_ref, carry_scr):
    @pl.when(pl.program_id(1) == 0)
    def _():
        carry_scr[...] = jnp.zeros_like(carry_scr)

    c = jnp.dot(tri_ref[...], lf_ref[0], precision=HIGHEST, preferred_element_type=F32) + carry_scr[...]
    carry_scr[...] = c[c.shape[0] - 1:, :]
    c2 = c * LOG2E
    for h in range(N_HEADS):
        crep_ref[0, h] = jnp.broadcast_to(c2[:, h:h + 1], c2.shape)
    ct_ref[0] = c2.T[0:8, :]


def _fox_cumsum(lf, tm):
    b, l, _ = lf.shape
    tri = jnp.asarray(np.tril(np.ones((tm, tm))), F32)
    return pl.pallas_call(
        _cumsum_kernel,
        grid=(b, l // tm),
        in_specs=[pl.BlockSpec((1, tm, LANES), lambda i, j: (i, j, 0)),
                  pl.BlockSpec((tm, tm), lambda i, j: (0, 0))],
        out_specs=[pl.BlockSpec((1, N_HEADS, tm, LANES), lambda i, j: (i, 0, j, 0)),
                   pl.BlockSpec((1, 8, tm), lambda i, j: (i, 0, j))],
        out_shape=[jax.ShapeDtypeStruct((b, N_HEADS, l, LANES), F32), jax.ShapeDtypeStruct((b, 8, l), F32)],
        scratch_shapes=[pltpu.VMEM((1, LANES), F32)],
        compiler_params=_cparams(("arbitrary", "arbitrary")),
        name="fox_cumsum",
    )(lf, tri)


def _fox_prompt_kernel(qt_ref, k_ref, vt_ref, crep_ref, ct_ref, o_ref, qm_scr, m_scr, l_scr, acc_scr, *, tq):
    qi = pl.program_id(1)
    ki = pl.program_id(2)

    @pl.when(ki == 0)
    def _():
        qt = qt_ref[0] * (LOG2E * HEAD_DIM ** -0.5)
        row_head = lax.broadcasted_iota(jnp.int32, qt.shape, 0) // HEAD_DIM
        for h in range(N_HEADS):
            qm_scr[h] = jnp.where(row_head == h, qt, 0.0).astype(BF16)
        m_scr[...] = jnp.full_like(m_scr, -jnp.inf)
        l_scr[...] = jnp.zeros_like(l_scr)
        acc_scr[...] = jnp.zeros_like(acc_scr)

    def block(masked):
        k = k_ref[0].astype(BF16)
        for h in range(N_HEADS):
            rows = slice(h * HEAD_DIM, (h + 1) * HEAD_DIM)
            s = jnp.dot(k, qm_scr[h], preferred_element_type=F32)
            s = s - jnp.tile(crep_ref[0, h], (1, tq // LANES)) + ct_ref[0, h:h + 1, :]
            if masked:
                key = lax.broadcasted_iota(jnp.int32, (tq, tq), 0)
                qry = lax.broadcasted_iota(jnp.int32, (tq, tq), 1)
                s = jnp.where(key <= qry, s, -jnp.inf)
            m_prev = m_scr[h:h + 1, :]
            m_new = jnp.maximum(m_prev, jnp.max(s, axis=0, keepdims=True))
            alpha = jnp.exp2(m_prev - m_new)
            p = jnp.exp2(s - m_new)
            l_scr[h:h + 1, :] = alpha * l_scr[h:h + 1, :] + jnp.sum(p, axis=0, keepdims=True)
            acc_scr[rows, :] = alpha * acc_scr[rows, :] + jnp.dot(vt_ref[0, rows, :].astype(BF16), p.astype(BF16),
                                                                  preferred_element_type=F32)
            m_scr[h:h + 1, :] = m_new

    @pl.when(ki < qi)
    def _():
        block(False)

    @pl.when(ki == qi)
    def _():
        block(True)
        for h in range(N_HEADS):
            rows = slice(h * HEAD_DIM, (h + 1) * HEAD_DIM)
            acc_scr[rows, :] = acc_scr[rows, :] / l_scr[h:h + 1, :]
        o_ref[0] = acc_scr[...].T


def _fox_prompt(fq, fk, fv, crep, ct, tq):
    b, l, _ = fq.shape
    n = l // tq
    qt = jnp.swapaxes(fq, 1, 2)
    vt = jnp.swapaxes(fv, 1, 2)
    kv = lambda i, j, k: jnp.minimum(k, j)
    return pl.pallas_call(
        functools.partial(_fox_prompt_kernel, tq=tq),
        grid=(b, n, n),
        in_specs=[pl.BlockSpec((1, GROUP_WIDTH, tq), lambda i, j, k: (i, 0, j)),
                  pl.BlockSpec((1, tq, GROUP_WIDTH), lambda i, j, k: (i, kv(i, j, k), 0)),
                  pl.BlockSpec((1, GROUP_WIDTH, tq), lambda i, j, k: (i, 0, kv(i, j, k))),
                  pl.BlockSpec((1, N_HEADS, tq, LANES), lambda i, j, k: (i, 0, kv(i, j, k), 0)),
                  pl.BlockSpec((1, 8, tq), lambda i, j, k: (i, 0, j))],
        out_specs=pl.BlockSpec((1, tq, GROUP_WIDTH), lambda i, j, k: (i, j, 0)),
        out_shape=jax.ShapeDtypeStruct((b, l, GROUP_WIDTH), F32),
        scratch_shapes=[pltpu.VMEM((N_HEADS, GROUP_WIDTH, tq), BF16), pltpu.VMEM((8, tq), F32),
                        pltpu.VMEM((8, tq), F32), pltpu.VMEM((GROUP_WIDTH, tq), F32)],
        compiler_params=_cparams(("arbitrary", "arbitrary", "arbitrary")),
        name="fox_prompt",
    )(qt, fk, vt, crep, ct)


PAGES_PER_CHUNK = 32
CHUNK_ROWS = PAGES_PER_CHUNK * PAGE_SIZE


def _fox_paged_kernel(pt_ref, q_ref, kn_ref, vn_ref, lfn_ref, mc_ref, mtot_ref, trix_ref,
                      ck_hbm, cv_hbm, clf_hbm, o_ref,
                      lf_scr, c_scr, kbuf, vbuf, kn_scr, vn_scr, m_scr, l_scr, acc_scr, sem_lf, sem_kv,
                      *, n_pages, page0):
    bi = pl.program_id(0)
    n_chunks = n_pages // PAGES_PER_CHUNK
    n_new = q_ref.shape[1]
    n_rows = N_HEADS * n_new

    def lf_copy(j):
        pg = pt_ref[bi, j] + page0
        return pltpu.make_async_copy(clf_hbm.at[pl.ds(pg, 1), :], lf_scr.at[pl.ds(j, 1), :], sem_lf.at[0])

    def kv_copies(chunk, slot, p):
        pg = pt_ref[bi, chunk * PAGES_PER_CHUNK + p] + page0
        toks = pl.ds(p * PAGE_SIZE, PAGE_SIZE)
        return (pltpu.make_async_copy(ck_hbm.at[pg], kbuf.at[slot, :, :, toks], sem_kv.at[0, slot]),
                pltpu.make_async_copy(cv_hbm.at[pg], vbuf.at[slot, :, :, toks], sem_kv.at[1, slot]))

    def start_chunk(chunk, slot):
        for p in range(PAGES_PER_CHUNK):
            for cp in kv_copies(chunk, slot, p):
                cp.start()

    def wait_chunk(chunk, slot):
        for p in range(PAGES_PER_CHUNK):
            for cp in kv_copies(chunk, slot, p):
                cp.wait()

    def lf_start(j, carry):
        lf_copy(j).start()
        return carry

    def lf_wait(j, carry):
        lf_copy(j).wait()
        return carry

    lax.fori_loop(0, n_pages, lf_start, 0)
    start_chunk(0, 0)

    lf_scr[n_pages:, :] = jnp.zeros((lf_scr.shape[0] - n_pages, lf_scr.shape[1]), F32)
    lf_scr[n_pages:n_pages + 1, :] = lfn_ref[0]
    kn_scr[...] = jnp.zeros_like(kn_scr)
    vn_scr[...] = jnp.zeros_like(vn_scr)
    kn_scr[0:n_new, :] = kn_ref[0]
    vn_scr[0:n_new, :] = vn_ref[0]
    m_scr[...] = jnp.full_like(m_scr, -jnp.inf)
    l_scr[...] = jnp.zeros_like(l_scr)
    acc_scr[...] = jnp.zeros_like(acc_scr)

    lax.fori_loop(0, n_pages, lf_wait, 0)
    lf = lf_scr[...]
    c_local = jnp.dot(lf, mc_ref[...], precision=HIGHEST, preferred_element_type=F32)
    tot = jnp.dot(lf, mtot_ref[...], precision=HIGHEST, preferred_element_type=F32)
    c2d = c_local + jnp.dot(trix_ref[...], tot, precision=HIGHEST, preferred_element_type=F32)
    for j in range(n_pages + 1):
        for h in range(N_HEADS):
            c_scr[h:h + 1, j * PAGE_SIZE:(j + 1) * PAGE_SIZE] = c2d[j:j + 1, h * PAGE_SIZE:(h + 1) * PAGE_SIZE]

    q4 = q_ref[0] * (HEAD_DIM ** -0.5)
    knt = kn_scr[...].T
    vnt = vn_scr[...].T
    lane = lax.broadcasted_iota(jnp.int32, (n_new, PAGE_SIZE), 1)
    tok = lax.broadcasted_iota(jnp.int32, (n_new, PAGE_SIZE), 0)
    new0 = n_pages * PAGE_SIZE

    def c_keys(h, start, width):
        return jnp.broadcast_to(c_scr[h:h + 1, pl.ds(start, width)], (n_new, width))

    def split(a):
        hi = a.astype(BF16)
        return hi, (a - hi.astype(F32)).astype(BF16)

    def mm3(a, b, dims):
        a_hi, a_lo = split(a)
        b_hi, b_lo = split(b)
        n = a.shape[0]
        both = lax.dot_general(jnp.concatenate([a_hi, a_lo], axis=0), b_hi, dims, preferred_element_type=F32)
        return both[0:n] + both[n:2 * n] + lax.dot_general(a_hi, b_lo, dims, preferred_element_type=F32)

    def attend(kt, vt, start, width, masked):
        for h in range(N_HEADS):
            r = slice(h * n_new, (h + 1) * n_new)
            d = slice(h * HEAD_DIM, (h + 1) * HEAD_DIM)
            cq = jnp.sum(jnp.where(lane == tok, c_keys(h, new0, PAGE_SIZE), 0.0), axis=-1, keepdims=True)
            s = mm3(q4[:, d], kt(h), _NN) + cq - c_keys(h, start, width)
            if masked:
                s = jnp.where(lane <= tok, s, -jnp.inf)
            m_prev = m_scr[r]
            m_new = jnp.maximum(m_prev, jnp.max(s, axis=-1, keepdims=True))
            alpha = jnp.exp(m_prev - m_new)
            p = jnp.exp(s - m_new)
            l_scr[r] = alpha * l_scr[r] + jnp.sum(p, axis=-1, keepdims=True)
            acc_scr[r] = alpha * acc_scr[r] + mm3(p, vt(h), _NT)
            m_scr[r] = m_new

    def chunk_body(chunk, carry):
        slot = chunk % 2
        wait_chunk(chunk, slot)

        @pl.when(chunk + 1 < n_chunks)
        def _():
            start_chunk(chunk + 1, 1 - slot)

        start = pl.multiple_of(chunk * CHUNK_ROWS, CHUNK_ROWS)
        attend(lambda h: kbuf[slot, h], lambda h: vbuf[slot, h], start, CHUNK_ROWS, False)
        return carry

    lax.fori_loop(0, n_chunks, chunk_body, 0)
    head_rows = lambda a: (lambda h: a[h * HEAD_DIM:(h + 1) * HEAD_DIM, :])
    attend(head_rows(knt), head_rows(vnt), new0, PAGE_SIZE, True)

    for h in range(N_HEADS):
        r = slice(h * n_new, (h + 1) * n_new)
        o_ref[0, :, h * HEAD_DIM:(h + 1) * HEAD_DIM] = acc_scr[r] / l_scr[r]


def _fox_paged(page_table, fq, fk, fv, lf_new_row, cache_k, cache_v, cache_lf, layer):
    s, n_new, _ = fq.shape
    n_pages = page_table.shape[1]
    n_pool = cache_k.shape[0] // 2
    n_rows = N_HEADS * n_new
    c_rows = ((n_pages + 1 + 7) // 8) * 8
    lanes = PAGE_SIZE * N_HEADS
    src = np.arange(lanes)
    dst = np.arange(lanes)
    same_head = (src[:, None] % N_HEADS) == (dst[None, :] // PAGE_SIZE)
    mc = jnp.asarray(same_head & ((src[:, None] // N_HEADS) <= (dst[None, :] % PAGE_SIZE)), F32)
    mtot = jnp.asarray(same_head, F32)
    trix = jnp.asarray(np.tril(np.ones((c_rows, c_rows)), -1), F32)
    tokspec = lambda w: pl.BlockSpec((1, n_new, w), lambda i, pt: (i, 0, 0))
    full = lambda a: pl.BlockSpec(a.shape, lambda i, pt: (0,) * a.ndim)
    anyspec = pl.BlockSpec(memory_space=pl.ANY)
    grid_spec = pltpu.PrefetchScalarGridSpec(
        num_scalar_prefetch=1,
        grid=(s,),
        in_specs=[tokspec(GROUP_WIDTH), tokspec(GROUP_WIDTH), tokspec(GROUP_WIDTH),
                  pl.BlockSpec((1, 1, lanes), lambda i, pt: (i, 0, 0)),
                  full(mc), full(mtot), full(trix), anyspec, anyspec, anyspec],
        out_specs=tokspec(GROUP_WIDTH),
        scratch_shapes=[pltpu.VMEM((c_rows, lanes), F32), pltpu.VMEM((8, (n_pages + 1) * PAGE_SIZE), F32),
                        pltpu.VMEM((2, N_HEADS, HEAD_DIM, CHUNK_ROWS), F32),
                        pltpu.VMEM((2, N_HEADS, HEAD_DIM, CHUNK_ROWS), F32),
                        pltpu.VMEM((PAGE_SIZE, GROUP_WIDTH), F32), pltpu.VMEM((PAGE_SIZE, GROUP_WIDTH), F32),
                        pltpu.VMEM((n_rows, 1), F32), pltpu.VMEM((n_rows, 1), F32),
                        pltpu.VMEM((n_rows, HEAD_DIM), F32),
                        pltpu.SemaphoreType.DMA((1,)), pltpu.SemaphoreType.DMA((2, 2))],
    )
    return pl.pallas_call(
        functools.partial(_fox_paged_kernel, n_pages=n_pages, page0=layer * n_pool),
        grid_spec=grid_spec,
        out_shape=jax.ShapeDtypeStruct((s, n_new, GROUP_WIDTH), F32),
        compiler_params=_cparams(("arbitrary",)),
        name="fox_paged",
    )(page_table, fq, fk, fv, lf_new_row, mc, mtot, trix, cache_k, cache_v, cache_lf)


def _convpool_kernel(u_ref, pu_ref, cpre_ref, ppre_ref, cw_ref, cb_ref, lng_ref, lnb_ref, pw_ref, ps_ref,
                     oc_ref, op_ref, cext, pext, *, tm, pos0, precise):
    i = pl.program_id(1)

    @pl.when(i == 0)
    def _():
        cext[0:CONV_HALO, :] = cpre_ref[0]
        pext[0:POOL_MAX, :] = ppre_ref[0]

    @pl.when(i > 0)
    def _():
        cext[0:CONV_HALO, :] = cext[tm:tm + CONV_HALO, :]
        pext[0:POOL_MAX, :] = pext[tm:tm + POOL_MAX, :]

    cext[CONV_HALO:CONV_HALO + tm, :] = u_ref[0]
    pext[POOL_MAX:POOL_MAX + tm, :] = pu_ref[0]

    off = CONV_HALO - (CONV_WIDTH - 1)
    y = cext[off:off + tm, :] * cw_ref[0:1, :]
    for j in range(1, CONV_WIDTH):
        y = y + cext[off + j:off + j + tm, :] * cw_ref[j:j + 1, :]
    y = y + cb_ref[...]
    mu = jnp.mean(y, axis=-1, keepdims=True)
    yc = y - mu
    var = jnp.mean(yc * yc, axis=-1, keepdims=True)
    yn = yc * lax.rsqrt(var + EPS) * lng_ref[...] + lnb_ref[...]
    oc_ref[0] = yn * jax.nn.sigmoid(yn)

    lane_group = lax.broadcasted_iota(jnp.int32, (tm, GROUP_WIDTH), 1) // (GROUP_WIDTH // len(POOL_WINDOWS))
    pos = pos0 + i * tm + lax.broadcasted_iota(jnp.int32, (tm, 1), 0)
    x0 = pext[POOL_MAX:POOL_MAX + tm, :]
    run = x0
    mean = jnp.zeros((tm, GROUP_WIDTH), F32)
    for back in range(1, POOL_MAX):
        run = run + pext[POOL_MAX - back:POOL_MAX - back + tm, :]
        if back + 1 in POOL_WINDOWS:
            g = POOL_WINDOWS.index(back + 1)
            cnt = jnp.minimum(back + 1, pos + 1).astype(F32)
            mean = jnp.where(lane_group == g, run / cnt, mean)
    d = mean - x0
    op_ref[0] = _mm(d, pw_ref[...], precise) * ps_ref[...]


def _convpool(u, pu, conv_prefix, pool_prefix, lp, tm, pos0, precise):
    b, l, _ = u.shape
    seq = pl.BlockSpec((1, tm, GROUP_WIDTH), lambda i, j: (i, j, 0))
    full = lambda a: pl.BlockSpec(a.shape, lambda i, j: (0,) * a.ndim)
    consts = (lp['conv_w'], lp['conv_b'], lp['conv_ln_g'], lp['conv_ln_b'],
              lp['pool_w_f32'] if precise else lp['pool_w'], lp['pool_scale'])
    return pl.pallas_call(
        functools.partial(_convpool_kernel, tm=tm, pos0=pos0, precise=precise),
        grid=(b, l // tm),
        in_specs=[seq, seq,
                  pl.BlockSpec((1, CONV_HALO, GROUP_WIDTH), lambda i, j: (i, 0, 0)),
                  pl.BlockSpec((1, POOL_MAX, GROUP_WIDTH), lambda i, j: (i, 0, 0))] + [full(a) for a in consts],
        out_specs=[seq, seq],
        out_shape=[jax.ShapeDtypeStruct((b, l, GROUP_WIDTH), F32)] * 2,
        scratch_shapes=[pltpu.VMEM((CONV_HALO + tm, GROUP_WIDTH), F32), pltpu.VMEM((POOL_MAX + tm, GROUP_WIDTH), F32)],
        compiler_params=_cparams(("arbitrary", "arbitrary")),
        name="convpool",
    )(u, pu, conv_prefix, pool_prefix, *consts)


def _router_logits(h2, wr):
    return jnp.dot(h2, wr, precision=HIGHEST, preferred_element_type=F32)


def _outproj_kernel(x_ref, og_ref, of_ref, oc_ref, op_ref, w_ref, g2_ref, wr_ref, br_ref,
                    x1_ref, h2_ref, lg_ref, *, precise):
    acc = x_ref[...]
    for j, m_ref in enumerate((og_ref, of_ref, oc_ref, op_ref)):
        acc = acc + _mm(m_ref[...], w_ref[j * GROUP_WIDTH:(j + 1) * GROUP_WIDTH, :], precise)
    x1_ref[...] = acc
    ms = jnp.mean(acc * acc, axis=-1, keepdims=True)
    h2 = acc * lax.rsqrt(ms + EPS) * g2_ref[...]
    h2_ref[...] = h2.astype(h2_ref.dtype)
    lg_ref[...] = _router_logits(h2, wr_ref[...]) + br_ref[...]


def _outproj(x2d, og, of, oc, op, lp, tm, precise):
    t = x2d.shape[0]
    row = lambda w: pl.BlockSpec((tm, w), lambda i: (i, 0))
    full = lambda a: pl.BlockSpec(a.shape, lambda i: (0,) * a.ndim)
    consts = (lp['w_out_f32'] if precise else lp['w_out'], lp['g2'], lp['w_router'], lp['b_router'])
    return pl.pallas_call(
        functools.partial(_outproj_kernel, precise=precise),
        grid=(t // tm,),
        in_specs=[row(D_MODEL)] + [row(GROUP_WIDTH)] * 4 + [full(a) for a in consts],
        out_specs=[row(D_MODEL), row(D_MODEL), row(LANES)],
        out_shape=[jax.ShapeDtypeStruct((t, D_MODEL), F32),
                   jax.ShapeDtypeStruct((t, D_MODEL), F32 if precise else BF16),
                   jax.ShapeDtypeStruct((t, LANES), F32)],
        compiler_params=_cparams(("arbitrary",)),
        name="outproj",
    )(x2d, og, of, oc, op, *consts)


def _route(lg):
    lane = lax.broadcasted_iota(jnp.int32, lg.shape, 1)
    big = jnp.int32(LANES)
    neg = -jnp.inf
    is_g = lane < MOE_GROUPS
    gl = jnp.where(is_g, lg, neg)
    gmax = jnp.max(gl, axis=-1, keepdims=True)
    gidx = jnp.min(jnp.where(gl == gmax, lane, big), axis=-1, keepdims=True)
    p_group = 1.0 / jnp.sum(jnp.where(is_g, jnp.exp(lg - gmax), 0.0), axis=-1, keepdims=True)
    lo = ROUTER_LANE0 + gidx * EXPERTS_PER_GROUP
    el = jnp.where((lane >= lo) & (lane < lo + EXPERTS_PER_GROUP), lg, neg)
    v1 = jnp.max(el, axis=-1, keepdims=True)
    i1 = jnp.min(jnp.where(el == v1, lane, big), axis=-1, keepdims=True)
    el2 = jnp.where(lane == i1, neg, el)
    v2 = jnp.max(el2, axis=-1, keepdims=True)
    i2 = jnp.min(jnp.where(el2 == v2, lane, big), axis=-1, keepdims=True)
    e21 = jnp.exp(v2 - v1)
    g1 = 1.0 / (1.0 + e21)
    g2 = e21 * g1
    return jnp.where(lane == i1, g1 * p_group, 0.0) + jnp.where(lane == i2, g2 * p_group, 0.0)


def _moe_dense_kernel(h_ref, lg_ref, x1_ref, wg_ref, wu_ref, wd_ref, o_ref, comb_scr, acc_scr, *, precise):
    e = pl.program_id(1)

    @pl.when(e == 0)
    def _():
        comb_scr[...] = _route(lg_ref[...])
        acc_scr[...] = x1_ref[...]

    h = h_ref[...]
    a = _mm(h, wg_ref[0], precise)
    b = _mm(h, wu_ref[0], precise)
    y = _mm(a * jax.nn.sigmoid(a) * b, wd_ref[0], precise)
    lane = lax.broadcasted_iota(jnp.int32, comb_scr.shape, 1)
    cw = jnp.sum(jnp.where(lane == e + ROUTER_LANE0, comb_scr[...], 0.0), axis=-1, keepdims=True)
    acc_scr[...] = acc_scr[...] + cw * y

    @pl.when(e == N_EXPERTS - 1)
    def _():
        o_ref[...] = acc_scr[...]


def _moe_dense(h2, lg, x1, lp, tm, precise):
    t = h2.shape[0]
    row = lambda w: pl.BlockSpec((tm, w), lambda i, e: (i, 0))
    sfx = '_f32' if precise else ''
    return pl.pallas_call(
        functools.partial(_moe_dense_kernel, precise=precise),
        grid=(t // tm, N_EXPERTS),
        in_specs=[row(D_MODEL), row(LANES), row(D_MODEL),
                  pl.BlockSpec((1, D_MODEL, D_EXPERT), lambda i, e: (e, 0, 0)),
                  pl.BlockSpec((1, D_MODEL, D_EXPERT), lambda i, e: (e, 0, 0)),
                  pl.BlockSpec((1, D_EXPERT, D_MODEL), lambda i, e: (e, 0, 0))],
        out_specs=row(D_MODEL),
        out_shape=jax.ShapeDtypeStruct((t, D_MODEL), F32),
        scratch_shapes=[pltpu.VMEM((tm, LANES), F32), pltpu.VMEM((tm, D_MODEL), F32)],
        compiler_params=_cparams(("arbitrary", "arbitrary")),
        name="moe_dense",
    )(h2, lg, x1, lp['exp_w_gate' + sfx], lp['exp_w_up' + sfx], lp['exp_w_down' + sfx])


def _block_diag(blocks):
    n, r, c = blocks.shape
    out = jnp.zeros((n, r, n, c), blocks.dtype)
    for g in range(n):
        out = out.at[g, :, g, :].set(blocks[g])
    return out.reshape(n * r, n * c)


def _prep_layer(l, p):
    w = p['w_in'][l]
    o = np.cumsum((0, 256, 256, 256, 256, GLA_RANK, 256, 256, 256, N_HEADS, 256, 256, 256))
    small = jnp.concatenate([w[:, o[8]:o[9]], w[:, o[4]:o[5]],
                             jnp.zeros((D_MODEL, LANES - N_HEADS - GLA_RANK), F32)], axis=1)
    w_packed = jnp.concatenate([w[:, o[0]:o[4]], w[:, o[5]:o[8]], w[:, o[9]:o[12]], small], axis=1)
    wa2 = jnp.zeros((LANES, GROUP_WIDTH), F32).at[SMALL_GA0:SMALL_GA0 + GLA_RANK].set(p['gla_w_a2'][l])
    bf = jnp.zeros((1, LANES), F32).at[0, SMALL_FF0:SMALL_FF0 + N_HEADS].set(p['fox_b_f'][l])
    ones_bd = _block_diag(jnp.ones((N_HEADS, HEAD_DIM, HEAD_DIM), F32))
    w_router = jnp.concatenate([p['router_wg'][l], p['router_we'][l],
                                jnp.zeros((D_MODEL, LANES - MOE_GROUPS - N_EXPERTS), F32)], axis=1)
    b_router = jnp.concatenate([p['router_bg'][l], p['router_be'][l],
                                jnp.zeros((LANES - MOE_GROUPS - N_EXPERTS,), F32)])[None, :]
    conv_w = jnp.concatenate([p['conv_w'][l], jnp.zeros((CONV_HALO - CONV_WIDTH, GROUP_WIDTH), F32)], axis=0)
    row = lambda a: a.reshape(1, -1).astype(F32)
    return {
        'g1': row(p['norm1_g'][l]), 'w_in': w_packed.astype(BF16), 'wa2': wa2, 'ba': row(p['gla_b_a'][l]), 'bf': bf,
        'gq': row(jnp.tile(p['fox_q_norm_g'][l], N_HEADS)), 'gk': row(jnp.tile(p['fox_k_norm_g'][l], N_HEADS)),
        'bd_mean': ones_bd * (1.0 / HEAD_DIM), 'bd_ones': ones_bd, 'gla_gn': row(p['gla_norm_g'][l]),
        'conv_w': conv_w, 'conv_b': row(p['conv_b'][l]), 'conv_ln_g': row(p['conv_ln_g'][l]),
        'conv_ln_b': row(p['conv_ln_b'][l]), 'pool_w': _block_diag(p['pool_w'][l]).astype(BF16),
        'pool_scale': row(p['pool_scale'][l]), 'w_out': p['w_out'][l].astype(BF16), 'g2': row(p['norm2_g'][l]),
        'w_router': w_router, 'b_router': b_router,
        'exp_w_gate': p['exp_w_gate'][l].astype(BF16), 'exp_w_up': p['exp_w_up'][l].astype(BF16),
        'exp_w_down': p['exp_w_down'][l].astype(BF16),
        'w_in_f32': w_packed, 'pool_w_f32': _block_diag(p['pool_w'][l].astype(F32)), 'w_out_f32': p['w_out'][l],
        'exp_w_gate_f32': p['exp_w_gate'][l], 'exp_w_up_f32': p['exp_w_up'][l], 'exp_w_down_f32': p['exp_w_down'][l],
    }


def _pad_rows_front(a, rows):
    return jnp.pad(a.astype(F32), ((0, 0), (rows - a.shape[1], 0), (0, 0)))


def _flat(a):
    return a.reshape(-1, a.shape[-1])


def _layer_prompt(lp, xp, bsz, seq):
    tp = 512
    gla_p, la_p, fq_p, fk_p, fv_p, lf_p, u_p, pu_p = _inproj(xp, lp, tp, False)
    seq3 = lambda a: a.reshape(bsz, seq, a.shape[-1])
    og_p, st_p = _gla(seq3(gla_p), seq3(la_p), jnp.zeros((bsz, GROUP_WIDTH, GROUP_WIDTH), F32), lp, 128, False)
    c_p, ct_p = _fox_cumsum(seq3(lf_p), 512)
    of_p = _fox_prompt(seq3(fq_p), seq3(fk_p), seq3(fv_p), c_p, ct_p, 512)
    oc_p, op_p = _convpool(seq3(u_p), seq3(pu_p), jnp.zeros((bsz, CONV_HALO, GROUP_WIDTH), F32),
                           jnp.zeros((bsz, POOL_MAX, GROUP_WIDTH), F32), lp, 512, 0, False)
    x1_p, h2_p, lg_p = _outproj(xp, _flat(og_p), _flat(of_p), _flat(oc_p), _flat(op_p), lp, tp, False)
    xp_new = _moe_dense(h2_p, lg_p, x1_p, lp, 1024, False)
    heads = lambda a: a.reshape(bsz, seq, N_HEADS, HEAD_DIM)
    outs = (heads(fk_p), heads(fv_p), seq3(lf_p)[:, :, :N_HEADS], _bd_to_state(st_p),
            seq3(u_p)[:, -(CONV_WIDTH - 1):], seq3(pu_p)[:, -(POOL_MAX - 1):])
    return xp_new, outs


def _layer_sample(l, lp, xs, dec_b, dec_seq, page_table, caches, state_gla, state_conv, state_pool):
    cache_k, cache_v, cache_lf = caches
    past = page_table.shape[1] * PAGE_SIZE
    flat = _flat
    ts = xs.shape[0]
    gla_s, la_s, fq_s, fk_s, fv_s, lf_s, u_s, pu_s = _inproj(xs, lp, ts, True)
    dec3 = lambda a: a.reshape(dec_b, dec_seq, a.shape[-1])
    pad_t = lambda a: jnp.pad(dec3(a), ((0, 0), (0, GLA_SUB - dec_seq), (0, 0)))
    og_s, st_s = _gla(pad_t(gla_s), pad_t(la_s), _state_to_bd(state_gla[l]), lp, GLA_SUB, True)
    og_s = og_s[:, :dec_seq]
    lf_row = jnp.pad(lf_s[:, :N_HEADS].reshape(dec_b, 1, dec_seq * N_HEADS),
                     ((0, 0), (0, 0), (0, PAGE_SIZE * N_HEADS - dec_seq * N_HEADS)))
    of_s = _fox_paged(page_table, dec3(fq_s), dec3(fk_s), dec3(fv_s), lf_row, cache_k, cache_v, cache_lf, l)
    oc_s, op_s = _convpool(dec3(u_s), dec3(pu_s), _pad_rows_front(state_conv[l], CONV_HALO),
                           _pad_rows_front(state_pool[l], POOL_MAX), lp, dec_seq, past, True)
    x1_s, h2_s, lg_s = _outproj(xs, flat(og_s), flat(of_s), flat(oc_s), flat(op_s), lp, ts, True)
    xs_new = _moe_dense(h2_s, lg_s, x1_s, lp, ts, True)
    heads = lambda a: a.reshape(dec_b, dec_seq, N_HEADS, HEAD_DIM)
    conv_s = jnp.concatenate([state_conv[l].astype(F32), dec3(u_s)], axis=1)[:, -(CONV_WIDTH - 1):]
    pool_s = jnp.concatenate([state_pool[l].astype(F32), dec3(pu_s)], axis=1)[:, -(POOL_MAX - 1):]
    outs = (heads(fk_s), heads(fv_s), dec3(lf_s)[:, :, :N_HEADS], _bd_to_state(st_s), conv_s, pool_s)
    return xs_new, outs


def kernel(x_prompt, x_sample, cache_k, cache_v, cache_logf, page_table, state_gla, state_conv, state_pool,
           norm1_g, w_in, gla_w_a2, gla_b_a, gla_norm_g, fox_b_f, fox_q_norm_g, fox_k_norm_g,
           conv_w, conv_b, conv_ln_g, conv_ln_b, pool_w, pool_scale, w_out, norm2_g,
           router_wg, router_bg, router_we, router_be, exp_w_gate, exp_w_up, exp_w_down):
    params = dict(norm1_g=norm1_g, w_in=w_in, gla_w_a2=gla_w_a2, gla_b_a=gla_b_a, gla_norm_g=gla_norm_g,
                  fox_b_f=fox_b_f, fox_q_norm_g=fox_q_norm_g, fox_k_norm_g=fox_k_norm_g, conv_w=conv_w,
                  conv_b=conv_b, conv_ln_g=conv_ln_g, conv_ln_b=conv_ln_b, pool_w=pool_w, pool_scale=pool_scale,
                  w_out=w_out, norm2_g=norm2_g, router_wg=router_wg, router_bg=router_bg, router_we=router_we,
                  router_be=router_be, exp_w_gate=exp_w_gate, exp_w_up=exp_w_up, exp_w_down=exp_w_down)
    depth = w_in.shape[0]
    bsz, seq, _ = x_prompt.shape
    dec_b, dec_seq, _ = x_sample.shape
    n_pool = cache_k.shape[1]
    paged = lambda c: jnp.transpose(c, (0, 1, 3, 4, 2)).reshape(depth * n_pool, N_HEADS, HEAD_DIM, PAGE_SIZE)
    caches = (paged(cache_k), paged(cache_v), cache_logf.astype(F32).reshape(depth * n_pool, PAGE_SIZE * N_HEADS))
    xp = x_prompt.reshape(bsz * seq, D_MODEL)
    xs = x_sample.reshape(dec_b * dec_seq, D_MODEL)
    per_layer = []
    for l in range(depth):
        lp = _prep_layer(l, params)
        xp, (kp, vp, lfp, glap, convp, poolp) = _layer_prompt(lp, xp, bsz, seq)
        xs, (ks, vs, lfs, glas, convs, pools) = _layer_sample(l, lp, xs, dec_b, dec_seq, page_table, caches,
                                                              state_gla, state_conv, state_pool)
        per_layer.append((kp, vp, lfp, ks, vs, lfs, glap, glas, convp, convs, poolp, pools))
    stacked = tuple(jnp.stack([per_layer[l][i] for l in range(depth)]) for i in range(12))
    return (xp.reshape(bsz, seq, D_MODEL), xs.reshape(dec_b, dec_seq, D_MODEL)) + stacked
```
